```python
import math
import jax
import jax.numpy as jnp
from jax import lax
import numpy as np

D_MODEL = 2048
BATCH = 8
SEQ = 4096
DEPTH = 1
DEC_BATCH = 32
DEC_SEQ = 64
PAST_LEN = 1024

CHUNK = 64
SC_WIDTH = 1024
SC_CONV_W = 3
DN_HEADS = 8
DN_DK = 128
DN_DV = 128
DN_CONV_W = 4
DN_QKV = DN_HEADS * (2 * DN_DK + DN_DV)
DN_VW = DN_HEADS * DN_DV
IN_SPLITS = (SC_WIDTH, SC_WIDTH, SC_WIDTH, DN_QKV, DN_VW, DN_HEADS, DN_HEADS, D_MODEL, D_MODEL)
IN_WIDTH = 3 * SC_WIDTH + DN_QKV + DN_VW + 2 * DN_HEADS + 2 * D_MODEL
PEER_HEADS = 8
PEER_NKEYS = 128
PEER_EXPERTS = PEER_NKEYS * PEER_NKEYS
PEER_QDIM = 256
PEER_HALF = PEER_QDIM // 2
PEER_TOPK = 16
PEER_BLOCK = 128
EPS = 1e-6

kernel_name = "hybrid_shortconv_gdn_peer_stream_step"


def rms_norm(x, g):
    xf = x.astype(jnp.float32)
    y = xf * lax.rsqrt(jnp.mean(xf * xf, axis=-1, keepdims=True) + EPS)
    return (y * g.astype(jnp.float32)).astype(x.dtype)


def l2_norm(x):
    return x * lax.rsqrt(jnp.sum(x * x, axis=-1, keepdims=True) + EPS)


def split_cols(a, sizes):
    out, start = [], 0
    for s in sizes:
        out.append(a[..., start:start + s])
        start += s
    return out


def causal_dwconv(x, w, buf):
    width, t = w.shape[0], x.shape[1]
    xp = jnp.concatenate([buf.astype(x.dtype), x], axis=1)
    y = xp[:, 0:t] * w[0]
    for i in range(1, width):
        y = y + xp[:, i:i + t] * w[i]
    return y, xp[:, t:]


def gated_delta_chunked(q, k, v, beta, g, s0, chunk):
    b, t, h, _ = q.shape
    dv = v.shape[-1]
    n = t // chunk

    def blocks(a):
        a = a.reshape((b, n, chunk, h) + a.shape[3:])
        return jnp.moveaxis(a, 3, 1)

    q, k, v, beta, g = (blocks(a) for a in (q, k, v, beta, g))
    gc = jnp.cumsum(g, axis=-1)
    causal = jnp.tril(jnp.ones((chunk, chunk), dtype=bool))
    eye = jnp.eye(chunk, dtype=q.dtype)
    decay = jnp.exp(jnp.where(causal, gc[..., :, None] - gc[..., None, :], -jnp.inf))
    kb = k * beta[..., None]
    lower = jnp.einsum('bhnid,bhnjd->bhnij', kb, k) * decay * (1.0 - eye)
    tmat = lax.linalg.triangular_solve(lower + eye, jnp.broadcast_to(eye, lower.shape),
                                       left_side=True, lower=True, unit_diagonal=True)
    u = jnp.einsum('bhnij,bhnje->bhnie', tmat, v * beta[..., None])
    w = jnp.einsum('bhnij,bhnjd->bhnid', tmat, kb * jnp.exp(gc)[..., None])
    qk = jnp.einsum('bhnid,bhnjd->bhnij', q, k) * decay
    qg = q * jnp.exp(gc)[..., None]
    g_last = gc[..., -1]
    kd = k * jnp.exp(g_last[..., None] - gc)[..., None]

    def step(s, xs):
        qk_i, qg_i, kd_i, u_i, w_i, gl_i = xs
        v_new = u_i - jnp.einsum('bhcd,bhde->bhce', w_i, s)
        o_i = jnp.einsum('bhcd,bhde->bhce', qg_i, s) + jnp.einsum('bhij,bhje->bhie', qk_i, v_new)
        s = s * jnp.exp(gl_i)[..., None, None] + jnp.einsum('bhcd,bhce->bhde', kd_i, v_new)
        return s, o_i

    xs = tuple(jnp.moveaxis(a, 2, 0) for a in (qk, qg, kd, u, w, g_last))
    s_fin, o = lax.scan(step, s0, xs)
    o = jnp.transpose(o, (1, 0, 3, 2, 4)).reshape(b, t, h, dv)
    return o, s_fin


def token_mixers(xn, dn_state, dn_buf, sc_buf, w_in, sc_conv_w, dn_conv_w, a_log, dt_bias,
                 dn_norm_g, w_br_sc, w_br_dn, w_o, chunk):
    b, t, _ = xn.shape
    proj = xn @ w_in
    sc_b, sc_c, sc_h, dn_qkv, dn_z, dn_beta, dn_a, gate_sc, gate_dn = split_cols(proj, IN_SPLITS)
    sc_conv, sc_buf_new = causal_dwconv(sc_c * sc_h, sc_conv_w, sc_buf)
    y_sc = (sc_b * sc_conv) @ w_br_sc
    qkv, dn_buf_new = causal_dwconv(dn_qkv, dn_conv_w, dn_buf)
    qkv = jax.nn.silu(qkv).astype(jnp.float32)
    q, k, v = split_cols(qkv, (DN_HEADS * DN_DK, DN_HEADS * DN_DK, DN_VW))
    q = l2_norm(q.reshape(b, t, DN_HEADS, DN_DK)) * (DN_DK ** -0.5)
    k = l2_norm(k.reshape(b, t, DN_HEADS, DN_DK))
    v = v.reshape(b, t, DN_HEADS, DN_DV)
    beta = jax.nn.sigmoid(dn_beta.astype(jnp.float32))
    g = -jnp.exp(a_log.astype(jnp.float32)) * jax.nn.softplus(dn_a.astype(jnp.float32) + dt_bias.astype(jnp.float32))
    o, s_new = gated_delta_chunked(q, k, v, beta, g, dn_state.astype(jnp.float32), chunk)
    z = dn_z.astype(jnp.float32).reshape(b, t, DN_HEADS, DN_DV)
    o = rms_norm(o, dn_norm_g) * jax.nn.silu(z)
    y_dn = o.reshape(b, t, DN_VW).astype(xn.dtype) @ w_br_dn
    merged = jax.nn.sigmoid(gate_sc) * y_sc + jax.nn.sigmoid(gate_dn) * y_dn
    return merged @ w_o, s_new.astype(dn_state.dtype), dn_buf_new, sc_buf_new


def peer(x, w_q, sub_keys, u_tab, v_tab):
    n = x.shape[0]
    q = (x @ w_q).reshape(n, PEER_HEADS, 2, PEER_HALF)
    s = jnp.einsum('nhpd,hpkd->nhpk', q, sub_keys).astype(jnp.float32)
    top_s, top_i = lax.top_k(s, PEER_TOPK)
    cand_s = (top_s[:, :, 0, :, None] + top_s[:, :, 1, None, :]).reshape(n, PEER_HEADS, PEER_TOPK * PEER_TOPK)
    cand_i = (top_i[:, :, 0, :, None] * PEER_NKEYS + top_i[:, :, 1, None, :]).reshape(n, PEER_HEADS, PEER_TOPK * PEER_TOPK)
    best_s, pos = lax.top_k(cand_s, PEER_TOPK)
    idx = jnp.take_along_axis(cand_i, pos, axis=-1)
    gate = jax.nn.softmax(best_s, axis=-1).astype(x.dtype)
    pad = (-n) % PEER_BLOCK
    nb = (n + pad) // PEER_BLOCK
    xp = jnp.pad(x, ((0, pad), (0, 0))).reshape(nb, PEER_BLOCK, x.shape[1])
    ip = jnp.pad(idx, ((0, pad), (0, 0), (0, 0))).reshape(nb, PEER_BLOCK, PEER_HEADS, PEER_TOPK)
    gp = jnp.pad(gate, ((0, pad), (0, 0), (0, 0))).reshape(nb, PEER_BLOCK, PEER_HEADS, PEER_TOPK)

    def block_fn(args):
        xb, ib, gb = args
        hid = jax.nn.gelu(jnp.einsum('td,thkd->thk', xb, u_tab[ib]), approximate=False)
        return jnp.einsum('thk,thkd->td', hid * gb, v_tab[ib])

    out = lax.map(block_fn, (xp, ip, gp))
    return out.reshape(nb * PEER_BLOCK, x.shape[1])[:n]


def trunk_layer(h, dn_state, dn_buf, sc_buf, norm1_g, w_in, sc_conv_w, dn_conv_w, a_log, dt_bias,
                dn_norm_g, w_br_sc, w_br_dn, w_o, norm2_g, peer_wq, peer_keys, peer_u, peer_v, chunk):
    mix, s_new, dn_buf_new, sc_buf_new = token_mixers(
        rms_norm(h, norm1_g), dn_state, dn_buf, sc_buf, w_in, sc_conv_w, dn_conv_w, a_log, dt_bias,
        dn_norm_g, w_br_sc, w_br_dn, w_o, chunk)
    h = h + mix
    b, t, d = h.shape
    h = h + peer(rms_norm(h, norm2_g).reshape(b * t, d), peer_wq, peer_keys, peer_u, peer_v).reshape(b, t, d)
    return h, s_new, dn_buf_new, sc_buf_new


def setup_inputs(seed: int = 0) -> dict:
    key = jax.random.key(seed)
    ks = jax.random.split(key, 24)
    f32 = jnp.float32
    L = DEPTH

    def nrm(k, shape, scale):
        return jax.random.normal(k, shape, f32) * scale

    dt = jnp.exp(jax.random.uniform(ks[10], (L, DN_HEADS), f32, math.log(1e-3), math.log(1e-1)))
    return {
        "x_prompt": nrm(ks[0], (BATCH, SEQ, D_MODEL), 1.0),
        "x_sample": nrm(ks[1], (DEC_BATCH, DEC_SEQ, D_MODEL), 1.0),
        "state_dn": nrm(ks[2], (L, DEC_BATCH, DN_HEADS, DN_DK, DN_DV), 0.1),
        "state_dn_conv": nrm(ks[3], (L, DEC_BATCH, DN_CONV_W - 1, DN_QKV), 1.0),
        "state_sc_conv": nrm(ks[4], (L, DEC_BATCH, SC_CONV_W - 1, SC_WIDTH), 1.0),
        "norm1_g": 1.0 + nrm(ks[5], (L, D_MODEL), 0.01),
        "w_in": nrm(ks[6], (L, D_MODEL, IN_WIDTH), D_MODEL ** -0.5),
        "sc_conv_w": nrm(ks[7], (L, SC_CONV_W, SC_WIDTH), SC_CONV_W ** -0.5),
        "dn_conv_w": nrm(ks[8], (L, DN_CONV_W, DN_QKV), DN_CONV_W ** -0.5),
        "dn_a_log": jnp.log(jax.random.uniform(ks[9], (L, DN_HEADS), f32, 1.0, 16.0)),
        "dn_dt_bias": dt + jnp.log(-jnp.expm1(-dt)),
        "dn_norm_g": 1.0 + nrm(ks[11], (L, DN_DV), 0.01),
        "w_br_sc": nrm(ks[12], (L, SC_WIDTH, D_MODEL), SC_WIDTH ** -0.5),
        "w_br_dn": nrm(ks[13], (L, DN_VW, D_MODEL), DN_VW ** -0.5),
        "w_o": nrm(ks[14], (L, D_MODEL, D_MODEL), D_MODEL ** -0.5),
        "norm2_g": 1.0 + nrm(ks[15], (L, D_MODEL), 0.01),
        "peer_wq": nrm(ks[16], (L, D_MODEL, PEER_HEADS * PEER_QDIM), D_MODEL ** -0.5),
        "peer_keys": nrm(ks[17], (L, PEER_HEADS, 2, PEER_NKEYS, PEER_HALF), PEER_HALF ** -0.5),
        "peer_u": nrm(ks[18], (L, PEER_EXPERTS, D_MODEL), D_MODEL ** -0.5),
        "peer_v": nrm(ks[19], (L, PEER_EXPERTS, D_MODEL), PEER_HEADS ** -0.5),
        "final_norm_g": 1.0 + nrm(ks[20], (D_MODEL,), 0.01),
    }


def reference(x_prompt, x_sample, state_dn, state_dn_conv, state_sc_conv, norm1_g, w_in, sc_conv_w,
              dn_conv_w, dn_a_log, dn_dt_bias, dn_norm_g, w_br_sc, w_br_dn, w_o, norm2_g, peer_wq,
              peer_keys, peer_u, peer_v, final_norm_g):
    hp, hs = x_prompt, x_sample
    dn_p, dnc_p, scc_p, dn_s, dnc_s, scc_s = [], [], [], [], [], []
    for l in range(DEPTH):
        weights = (norm1_g[l], w_in[l], sc_conv_w[l], dn_conv_w[l], dn_a_log[l], dn_dt_bias[l],
                   dn_norm_g[l], w_br_sc[l], w_br_dn[l], w_o[l], norm2_g[l], peer_wq[l], peer_keys[l],
                   peer_u[l], peer_v[l])
        zero_dn = jnp.zeros((BATCH, DN_HEADS, DN_DK, DN_DV), state_dn.dtype)
        zero_dnc = jnp.zeros((BATCH, DN_CONV_W - 1, DN_QKV), hp.dtype)
        zero_scc = jnp.zeros((BATCH, SC_CONV_W - 1, SC_WIDTH), hp.dtype)
        hp, s1, c1, c2 = trunk_layer(hp, zero_dn, zero_dnc, zero_scc, *weights, CHUNK)
        hs, s3, c3, c4 = trunk_layer(hs, state_dn[l], state_dn_conv[l], state_sc_conv[l], *weights, x_sample.shape[1])
        dn_p.append(s1); dnc_p.append(c1); scc_p.append(c2)
        dn_s.append(s3); dnc_s.append(c3); scc_s.append(c4)
    y_prompt = rms_norm(hp, final_norm_g)
    y_sample = rms_norm(hs, final_norm_g)
    return (y_prompt, y_sample, jnp.stack(dn_p), jnp.stack(dnc_p), jnp.stack(scc_p),
            jnp.stack(dn_s), jnp.stack(dnc_s), jnp.stack(scc_s))
```

```python
import functools
import math

import jax
import jax.numpy as jnp
from jax import lax
from jax.experimental import pallas as pl
from jax.experimental.pallas import tpu as pltpu

F32 = jnp.float32
BF16 = jnp.bfloat16
EPS = 1e-6
LANES = 128
SUBLANES = 8
CHUNK = 64
DN_HEADS = 8
DN_DK = 128
DN_DV = 128
PEER_HEADS = 8
PEER_NKEYS = 128
PEER_HALF = 128
PEER_TOPK = 16
NEG_INF = float("-inf")
HIGHEST = lax.Precision.HIGHEST
MIB = 2 ** 20


def _params(sem, vmem_mib):
    return pltpu.CompilerParams(dimension_semantics=sem, vmem_limit_bytes=vmem_mib * MIB)


def _dot(a, b):
    return jnp.dot(a.astype(BF16), b.astype(BF16), preferred_element_type=F32)


def _dot_nt(a, b):
    return lax.dot_general(a.astype(BF16), b.astype(BF16), (((1,), (1,)), ((), ())),
                           preferred_element_type=F32)


def _dot_tn(a, b):
    return lax.dot_general(a.astype(BF16), b.astype(BF16), (((0,), (0,)), ((), ())),
                           preferred_element_type=F32)


def _dot_hi(a, b):
    return jnp.dot(a, b, preferred_element_type=F32, precision=HIGHEST)


def _rms(x, g):
    return x * lax.rsqrt(jnp.mean(x * x, axis=-1, keepdims=True) + EPS) * g


def _softplus(x):
    return jnp.maximum(x, 0.0) + jnp.log1p(jnp.exp(-jnp.abs(x)))


def _inproj_body(x_ref, g_ref, w_ref, wba_ref, proj_ref, ba_ref, xn_ref):
    @pl.when(pl.program_id(1) == 0)
    def _():
        xn = _rms(x_ref[...], g_ref[...]).astype(BF16)
        xn_ref[...] = xn
        ba_ref[...] = jnp.dot(xn, wba_ref[...], preferred_element_type=F32)

    proj_ref[...] = jnp.dot(xn_ref[...], w_ref[...], preferred_element_type=F32)


def _inproj(x, g, w_main, w_ba, tm, tn):
    n, d = x.shape
    width = w_main.shape[1]
    return pl.pallas_call(
        _inproj_body,
        grid=(n // tm, width // tn),
        in_specs=[
            pl.BlockSpec((tm, d), lambda i, j: (i, 0)),
            pl.BlockSpec((1, d), lambda i, j: (0, 0)),
            pl.BlockSpec((d, tn), lambda i, j: (0, j)),
            pl.BlockSpec((d, LANES), lambda i, j: (0, 0)),
        ],
        out_specs=[
            pl.BlockSpec((tm, tn), lambda i, j: (i, j)),
            pl.BlockSpec((tm, LANES), lambda i, j: (i, 0)),
        ],
        out_shape=[jax.ShapeDtypeStruct((n, width), F32), jax.ShapeDtypeStruct((n, LANES), F32)],
        scratch_shapes=[pltpu.VMEM((tm, d), BF16)],
        compiler_params=_params(("parallel", "arbitrary"), 48),
        name="inproj",
    )(x, g, w_main, w_ba)


def _prep_body(scb_ref, scc_ref, sch_ref, qkv_ref, ba_ref, scbuf_ref, dnbuf_ref, scw_ref, dnw_ref,
               alog_ref, dtb_ref,
               ysc_ref, q_ref, k_ref, v_ref, bx_ref, gx_ref, scnew_ref, dnnew_ref,
               ext_sc, ext_dn, *, sc_taps, dn_taps):
    tt = scb_ref.shape[0]
    halo = SUBLANES

    @pl.when(pl.program_id(1) == 0)
    def _():
        ext_sc[0:halo, :] = scbuf_ref[...]
        ext_dn[0:halo, :] = dnbuf_ref[...]

    ext_sc[halo:halo + tt, :] = scc_ref[...] * sch_ref[...]
    for cg in range(scb_ref.shape[1] // LANES):
        cs = slice(cg * LANES, (cg + 1) * LANES)
        acc = None
        for i in range(sc_taps):
            r0 = halo - (sc_taps - 1) + i
            term = ext_sc[r0:r0 + tt, cs] * scw_ref[i:i + 1, cs]
            acc = term if acc is None else acc + term
        ysc_ref[:, cs] = (scb_ref[:, cs] * acc).astype(BF16)
    scnew_ref[...] = ext_sc[tt:tt + halo, :]
    ext_sc[0:halo, :] = ext_sc[tt:tt + halo, :]

    ext_dn[halo:halo + tt, :] = qkv_ref[...]
    n_qk = DN_HEADS * DN_DK // LANES
    for cg in range(qkv_ref.shape[1] // LANES):
        cs = slice(cg * LANES, (cg + 1) * LANES)
        acc = None
        for i in range(dn_taps):
            r0 = halo - (dn_taps - 1) + i
            term = ext_dn[r0:r0 + tt, cs] * dnw_ref[i:i + 1, cs]
            acc = term if acc is None else acc + term
        a = acc * jax.nn.sigmoid(acc)
        if cg < 2 * n_qk:
            a = a * lax.rsqrt(jnp.sum(a * a, axis=-1, keepdims=True) + EPS)
        if cg < n_qk:
            q_ref[:, cs] = a * (DN_DK ** -0.5)
        elif cg < 2 * n_qk:
            k_ref[:, slice((cg - n_qk) * LANES, (cg - n_qk + 1) * LANES)] = a
        else:
            v_ref[:, slice((cg - 2 * n_qk) * LANES, (cg - 2 * n_qk + 1) * LANES)] = a
    dnnew_ref[...] = ext_dn[tt:tt + halo, :]
    ext_dn[0:halo, :] = ext_dn[tt:tt + halo, :]

    ba = ba_ref[...]
    beta = jax.nn.sigmoid(ba)
    g = -jnp.exp(alog_ref[...]) * _softplus(ba + dtb_ref[...])
    for h in range(DN_HEADS):
        cs = slice(h * LANES, (h + 1) * LANES)
        bx_ref[:, cs] = jnp.broadcast_to(beta[:, h:h + 1], (tt, LANES))
        gx_ref[:, cs] = jnp.broadcast_to(g[:, DN_HEADS + h:DN_HEADS + h + 1], (tt, LANES))


def _prep(proj, ba, scbuf, dnbuf, scw, dnw, alog, dtb, nb, t, tt, sc, qkv, col0_sc, sc_taps, dn_taps):
    n = nb * t
    nt = t // tt
    vw = DN_HEADS * DN_DV
    row = lambda b, s: b * nt + s
    scblk = col0_sc // sc
    body = functools.partial(_prep_body, sc_taps=sc_taps, dn_taps=dn_taps)
    return pl.pallas_call(
        body,
        grid=(nb, nt),
        in_specs=[
            pl.BlockSpec((tt, sc), lambda b, s: (row(b, s), scblk)),
            pl.BlockSpec((tt, sc), lambda b, s: (row(b, s), scblk + 1)),
            pl.BlockSpec((tt, sc), lambda b, s: (row(b, s), scblk + 2)),
            pl.BlockSpec((tt, qkv), lambda b, s: (row(b, s), 0)),
            pl.BlockSpec((tt, LANES), lambda b, s: (row(b, s), 0)),
            pl.BlockSpec((None, SUBLANES, sc), lambda b, s: (b, 0, 0)),
            pl.BlockSpec((None, SUBLANES, qkv), lambda b, s: (b, 0, 0)),
            pl.BlockSpec((SUBLANES, sc), lambda b, s: (0, 0)),
            pl.BlockSpec((SUBLANES, qkv), lambda b, s: (0, 0)),
            pl.BlockSpec((1, LANES), lambda b, s: (0, 0)),
            pl.BlockSpec((1, LANES), lambda b, s: (0, 0)),
        ],
        out_specs=[
            pl.BlockSpec((tt, sc), lambda b, s: (row(b, s), 0)),
            pl.BlockSpec((tt, vw), lambda b, s: (row(b, s), 0)),
            pl.BlockSpec((tt, vw), lambda b, s: (row(b, s), 0)),
            pl.BlockSpec((tt, vw), lambda b, s: (row(b, s), 0)),
            pl.BlockSpec((tt, vw), lambda b, s: (row(b, s), 0)),
            pl.BlockSpec((tt, vw), lambda b, s: (row(b, s), 0)),
            pl.BlockSpec((None, SUBLANES, sc), lambda b, s: (b, 0, 0)),
            pl.BlockSpec((None, SUBLANES, qkv), lambda b, s: (b, 0, 0)),
        ],
        out_shape=[
            jax.ShapeDtypeStruct((n, sc), BF16),
            jax.ShapeDtypeStruct((n, vw), F32),
            jax.ShapeDtypeStruct((n, vw), F32),
            jax.ShapeDtypeStruct((n, vw), F32),
            jax.ShapeDtypeStruct((n, vw), F32),
            jax.ShapeDtypeStruct((n, vw), F32),
            jax.ShapeDtypeStruct((nb, SUBLANES, sc), F32),
            jax.ShapeDtypeStruct((nb, SUBLANES, qkv), F32),
        ],
        scratch_shapes=[pltpu.VMEM((tt + SUBLANES, sc), F32), pltpu.VMEM((tt + SUBLANES, qkv), F32)],
        compiler_params=_params(("parallel", "arbitrary"), 48),
        name="prep",
    )(proj, proj, proj, proj, ba, scbuf, dnbuf, scw, dnw, alog, dtb)


def _delta_body(q_ref, k_ref, v_ref, bx_ref, gx_ref, s0_ref, o_ref, sout_ref, s_ref):
    c = pl.program_id(1)
    nc = pl.num_programs(1)

    @pl.when(c == 0)
    def _():
        s_ref[...] = s0_ref[...]

    ri = lax.broadcasted_iota(jnp.int32, (CHUNK, CHUNK), 0)
    ci = lax.broadcasted_iota(jnp.int32, (CHUNK, CHUNK), 1)
    causal = ri >= ci
    strict = ri > ci
    tril = causal.astype(F32)
    eye = (ri == ci).astype(F32)
    lane0 = (lax.broadcasted_iota(jnp.int32, (CHUNK, LANES), 1) == 0).astype(F32)

    gc_all = _dot_hi(tril, gx_ref[...])
    for h in range(DN_HEADS):
        cs = slice(h * LANES, (h + 1) * LANES)
        q = q_ref[:, cs]
        k = k_ref[:, cs]
        v = v_ref[:, cs]
        bx = bx_ref[:, cs]
        gc = gc_all[:, cs]
        gc_row = lax.dot_general(lane0, gc, (((1,), (1,)), ((), ())), preferred_element_type=F32,
                                 precision=HIGHEST)
        decay = jnp.exp(jnp.where(causal, gc[:, :CHUNK] - gc_row, NEG_INF))
        kb = k * bx
        low = jnp.where(strict, _dot_nt(kb, k) * decay, 0.0)
        m = -low
        tmat = eye + m
        for _ in range(int(math.log2(CHUNK)) - 1):
            m = _dot_hi(m, m)
            tmat = tmat + _dot_hi(tmat, m)
        egc = jnp.exp(gc)
        u = _dot(tmat, v * bx)
        w = _dot(tmat, kb * egc)
        qk = _dot_nt(q, k) * decay
        g_last = gc[CHUNK - 1:CHUNK, :]
        s = s_ref[h]
        v_new = u - _dot(w, s)
        o_ref[:, cs] = _dot(q * egc, s) + _dot(qk, v_new)
        s_ref[h] = s * jnp.exp(g_last) + _dot_tn(k * jnp.exp(g_last - gc), v_new)

    @pl.when(c == nc - 1)
    def _():
        sout_ref[...] = s_ref[...]


def _delta(q, k, v, bx, gx, s0, nb, t):
    n = nb * t
    nc = t // CHUNK
    vw = DN_HEADS * DN_DV
    tok = pl.BlockSpec((CHUNK, vw), lambda b, c: (b * nc + c, 0))
    st = pl.BlockSpec((None, DN_HEADS, DN_DK, DN_DV), lambda b, c: (b, 0, 0, 0))
    return pl.pallas_call(
        _delta_body,
        grid=(nb, nc),
        in_specs=[tok, tok, tok, tok, tok, st],
        out_specs=[tok, st],
        out_shape=[jax.ShapeDtypeStruct((n, vw), F32),
                   jax.ShapeDtypeStruct((nb, DN_HEADS, DN_DK, DN_DV), F32)],
        scratch_shapes=[pltpu.VMEM((DN_HEADS, DN_DK, DN_DV), F32)],
        compiler_params=_params(("parallel", "arbitrary"), 32),
        name="delta",
    )(q, k, v, bx, gx, s0)


def _merge_body(x_ref, ysc_ref, o_ref, z_ref, gsc_ref, gdn_ref, ng_ref, wsc_ref, wdn_ref, wo_ref, h_ref):
    tm = x_ref.shape[0]
    parts = []
    for h in range(DN_HEADS):
        cs = slice(h * LANES, (h + 1) * LANES)
        z = z_ref[:, cs]
        on = _rms(o_ref[:, cs], ng_ref[...]) * (z * jax.nn.sigmoid(z))
        parts.append(on.astype(BF16))
    y_dn = jnp.dot(jnp.concatenate(parts, axis=1), wdn_ref[...], preferred_element_type=F32)
    y_sc = jnp.dot(ysc_ref[...], wsc_ref[...], preferred_element_type=F32)
    merged = jax.nn.sigmoid(gsc_ref[...]) * y_sc + jax.nn.sigmoid(gdn_ref[...]) * y_dn
    h_ref[...] = x_ref[...] + jnp.dot(merged.astype(BF16), wo_ref[...], preferred_element_type=F32)


def _merge(x, ysc, o, proj, ng, wsc, wdn, wo, tm, col0_z, col0_gsc, col0_gdn):
    n, d = x.shape
    sc = ysc.shape[1]
    vw = o.shape[1]
    const = lambda shape: pl.BlockSpec(shape, lambda i: (0, 0), pipeline_mode=pl.Buffered(1))
    return pl.pallas_call(
        _merge_body,
        grid=(n // tm,),
        in_specs=[
            pl.BlockSpec((tm, d), lambda i: (i, 0)),
            pl.BlockSpec((tm, sc), lambda i: (i, 0)),
            pl.BlockSpec((tm, vw), lambda i: (i, 0)),
            pl.BlockSpec((tm, vw), lambda i: (i, col0_z // vw)),
            pl.BlockSpec((tm, d), lambda i: (i, col0_gsc // d)),
            pl.BlockSpec((tm, d), lambda i: (i, col0_gdn // d)),
            const((1, LANES)),
            const((sc, d)),
            const((vw, d)),
            const((d, d)),
        ],
        out_specs=pl.BlockSpec((tm, d), lambda i: (i, 0)),
        out_shape=jax.ShapeDtypeStruct((n, d), F32),
        compiler_params=_params(("parallel",), 48),
        name="merge",
    )(x, ysc, o, proj, proj, proj, ng, wsc, wdn, wo)


_CAND_ROWS = 10 * SUBLANES


def _topk_ranks(s):
    nk, ln = s.shape
    row = lax.broadcasted_iota(jnp.int32, (nk, ln), 0)
    row16 = lax.broadcasted_iota(jnp.int32, (PEER_TOPK, ln), 0)

    def step(it, carry):
        s, rank, vals = carry
        m = jnp.max(s, axis=0, keepdims=True)
        first = jnp.min(jnp.where(s == m, row, nk), axis=0, keepdims=True)
        sel = row == first
        rank = jnp.where(sel, it.astype(F32), rank)
        s = jnp.where(sel, NEG_INF, s)
        vals = jnp.where(row16 == it, m, vals)
        return s, rank, vals

    init = (s, jnp.full((nk, ln), float(PEER_TOPK), F32), jnp.zeros((PEER_TOPK, ln), F32))
    _, rank, vals = lax.fori_loop(0, PEER_TOPK, step, init)
    return rank, vals


def _joint_select(v1, v2):
    ln = v1.shape[1]
    sub = lax.broadcasted_iota(jnp.int32, (SUBLANES, ln), 0)
    pieces, pos = [], []
    for r in range(SUBLANES):
        cnt = min(SUBLANES, PEER_TOPK // (r + 1))
        blk = v1[r:r + 1, :] + v2[0:SUBLANES, :]
        pieces.append(jnp.where(sub < cnt, blk, NEG_INF))
        pos.append(r * PEER_TOPK + sub)
    pieces.append(v1[0:1, :] + v2[SUBLANES:, :])
    pos.append(SUBLANES + sub)
    pieces.append(v1[SUBLANES:, :] + v2[0:1, :])
    pos.append((SUBLANES + sub) * PEER_TOPK)
    cand = jnp.concatenate(pieces, axis=0)
    pos = jnp.concatenate(pos, axis=0)
    big = PEER_TOPK * PEER_TOPK

    def step(_, carry):
        cand, chosen = carry
        m = jnp.max(cand, axis=0, keepdims=True)
        first = jnp.min(jnp.where(cand == m, pos, big), axis=0, keepdims=True)
        sel = pos == first
        return jnp.where(sel, NEG_INF, cand), jnp.where(sel, 1.0, chosen)

    _, chosen = lax.fori_loop(0, PEER_TOPK, step, (cand, jnp.zeros((_CAND_ROWS, ln), F32)))
    return chosen


def _router_body(h_ref, g_ref, wqt_ref, keys_ref, xnt_ref, a_ref, b_ref, c_ref, r_ref, qt_ref, st_ref, rk_ref, vl_ref):
    tm = h_ref.shape[0]
    xn = _rms(h_ref[...], g_ref[...])
    xnt = xn.T.astype(BF16)
    xnt_ref[...] = xnt
    qt_ref[...] = jnp.dot(wqt_ref[...], xnt, preferred_element_type=F32).astype(BF16)

    def score_step(hp, carry):
        q = qt_ref[pl.ds(pl.multiple_of(hp * PEER_HALF, PEER_HALF), PEER_HALF), :]
        st_ref[hp] = jnp.dot(keys_ref[hp], q, preferred_element_type=F32)
        return carry

    lax.fori_loop(0, 2 * PEER_HEADS, score_step, 0)

    for lc in range(tm // LANES):
        ls = slice(lc * LANES, (lc + 1) * LANES)

        def rank_step(hp, carry):
            rank, vals = _topk_ranks(st_ref[hp, :, ls])
            rk_ref[hp, :, ls] = rank
            vl_ref[hp, :, ls] = vals
            return carry

        lax.fori_loop(0, 2 * PEER_HEADS, rank_step, 0)

        def head_step(h, carry):
            v1 = vl_ref[2 * h, :, ls]
            v2 = vl_ref[2 * h + 1, :, ls]
            chosen = _joint_select(v1, v2)
            e1 = jnp.exp(v1 - v1[0:1, :])
            e2 = jnp.exp(v2 - v2[0:1, :])
            z = jnp.zeros((1, LANES), F32)
            c_rows = []
            for r in range(SUBLANES):
                blk = chosen[r * SUBLANES:(r + 1) * SUBLANES, :]
                cnt = jnp.sum(blk, axis=0, keepdims=True)
                z = z + e1[r:r + 1, :] * jnp.sum(blk * e2[0:SUBLANES, :], axis=0, keepdims=True)
                if r == 0:
                    strip = chosen[SUBLANES * SUBLANES:SUBLANES * SUBLANES + SUBLANES, :]
                    cnt = cnt + jnp.sum(strip, axis=0, keepdims=True)
                    z = z + e1[0:1, :] * jnp.sum(strip * e2[SUBLANES:, :], axis=0, keepdims=True)
                c_rows.append(cnt)
            strip = chosen[SUBLANES * SUBLANES + SUBLANES:, :]
            z = z + e2[0:1, :] * jnp.sum(strip * e1[SUBLANES:, :], axis=0, keepdims=True)
            counts = jnp.concatenate(c_rows + [strip], axis=0)
            inv_z = 1.0 / z

            rank1 = rk_ref[2 * h, :, ls]
            rank2 = rk_ref[2 * h + 1, :, ls]
            in1 = rank1 < float(PEER_TOPK)
            in2 = rank2 < float(PEER_TOPK)
            a_key = jnp.exp(jnp.where(in1, st_ref[2 * h, :, ls] - v1[0:1, :], NEG_INF)) * inv_z
            b_key = jnp.exp(jnp.where(in2, st_ref[2 * h + 1, :, ls] - v2[0:1, :], NEG_INF))
            c_key = jnp.zeros((PEER_NKEYS, LANES), F32)
            for r in range(PEER_TOPK):
                c_key = jnp.where(rank1 == float(r), counts[r:r + 1, :], c_key)
            a_ref[h, :, ls] = a_key
            c_ref[h, :, ls] = c_key
            b_ref[h, :, ls] = b_key.astype(BF16)
            r_ref[h, :, ls] = rank2.astype(BF16)
            return carry

        lax.fori_loop(0, PEER_HEADS, head_step, 0)


def _router(h, g, wqt, keys, tm):
    n, d = h.shape
    qd = wqt.shape[0]
    n_hp = 2 * PEER_HEADS
    fac = lambda: pl.BlockSpec((PEER_HEADS, PEER_NKEYS, tm), lambda i: (0, 0, i))
    fac_shape = lambda dt: jax.ShapeDtypeStruct((PEER_HEADS, PEER_NKEYS, n), dt)
    return pl.pallas_call(
        _router_body,
        grid=(n // tm,),
        in_specs=[
            pl.BlockSpec((tm, d), lambda i: (i, 0)),
            pl.BlockSpec((1, d), lambda i: (0, 0)),
            pl.BlockSpec((qd, d), lambda i: (0, 0), pipeline_mode=pl.Buffered(1)),
            pl.BlockSpec((n_hp, PEER_NKEYS, PEER_HALF), lambda i: (0, 0, 0), pipeline_mode=pl.Buffered(1)),
        ],
        out_specs=[pl.BlockSpec((d, tm), lambda i: (0, i)), fac(), fac(), fac(), fac()],
        out_shape=[jax.ShapeDtypeStruct((d, n), BF16), fac_shape(F32), fac_shape(BF16), fac_shape(F32),
                   fac_shape(BF16)],
        scratch_shapes=[
            pltpu.VMEM((qd, tm), BF16),
            pltpu.VMEM((n_hp, PEER_NKEYS, tm), F32),
            pltpu.VMEM((n_hp, PEER_NKEYS, tm), F32),
            pltpu.VMEM((n_hp, PEER_TOPK, tm), F32),
        ],
        compiler_params=_params(("parallel",), 48),
        name="router",
    )(h, g, wqt, keys)


def _experts_body(xnt_ref, u_ref, vt_ref, a_ref, b_ref, c_ref, r_ref, h_ref, g_ref, y_ref, acc_ref, w_ref, *,
                  final_norm):
    e = pl.program_id(1)
    n_i = u_ref.shape[0] // PEER_NKEYS
    tm = xnt_ref.shape[1]

    @pl.when(e == 0)
    def _():
        acc_ref[...] = jnp.zeros_like(acc_ref)

    hid = jnp.dot(u_ref[...], xnt_ref[...], preferred_element_type=F32)
    for ii in range(n_i):
        gate = jnp.zeros((PEER_NKEYS, tm), F32)
        for h in range(PEER_HEADS):
            a_i = a_ref[h, ii:ii + 1, :]
            c_i = c_ref[h, ii:ii + 1, :]
            gate = gate + jnp.where(r_ref[h].astype(F32) < c_i, a_i * b_ref[h].astype(F32), 0.0)
        x = hid[ii * PEER_NKEYS:(ii + 1) * PEER_NKEYS, :]
        act = 0.5 * x * (1.0 + lax.erf(x * (2.0 ** -0.5)))
        w_ref[ii * PEER_NKEYS:(ii + 1) * PEER_NKEYS, :] = (act * gate).astype(BF16)
    acc_ref[...] += jnp.dot(vt_ref[...], w_ref[...], preferred_element_type=F32)

    @pl.when(e == pl.num_programs(1) - 1)
    def _():
        y = h_ref[...] + acc_ref[...].T
        y_ref[...] = _rms(y, g_ref[...]) if final_norm else y


def _experts(xnt, u, vt, a, b, c, r, h, g, tm, te, final_norm):
    d, n = xnt.shape
    n_i = te // PEER_NKEYS
    full = lambda: pl.BlockSpec((PEER_HEADS, PEER_NKEYS, tm), lambda i, e: (0, 0, i))
    rows = lambda: pl.BlockSpec((PEER_HEADS, n_i, tm), lambda i, e: (0, e, i))
    return pl.pallas_call(
        functools.partial(_experts_body, final_norm=final_norm),
        grid=(n // tm, u.shape[0] // te),
        in_specs=[
            pl.BlockSpec((d, tm), lambda i, e: (0, i)),
            pl.BlockSpec((te, d), lambda i, e: (e, 0)),
            pl.BlockSpec((d, te), lambda i, e: (0, e)),
            rows(), full(), rows(), full(),
            pl.BlockSpec((tm, d), lambda i, e: (i, 0)),
            pl.BlockSpec((1, d), lambda i, e: (0, 0)),
        ],
        out_specs=pl.BlockSpec((tm, d), lambda i, e: (i, 0)),
        out_shape=jax.ShapeDtypeStruct((n, d), F32),
        scratch_shapes=[pltpu.VMEM((d, tm), F32), pltpu.VMEM((te, tm), BF16)],
        compiler_params=_params(("parallel", "arbitrary"), 56),
        name="experts",
    )(xnt, u, vt, a, b, c, r, h, g)


def _pick(n, pref):
    t = min(pref, n)
    while n % t:
        t //= 2
    return t


def _pad_rows_front(a, rows):
    return jnp.pad(a, ((0, 0), (rows - a.shape[1], 0), (0, 0)))


def _lane_row(vals, offset):
    return jnp.zeros((1, LANES), F32).at[0, offset:offset + vals.shape[0]].set(vals.astype(F32))


def _layer(hs, states, wts, final_g):
    (norm1_g, w_in, sc_conv_w, dn_conv_w, a_log, dt_bias, dn_norm_g, w_br_sc, w_br_dn, w_o, norm2_g,
     peer_wq, peer_keys, peer_u, peer_v) = wts
    d = w_in.shape[0]
    sc = sc_conv_w.shape[1]
    qkv = dn_conv_w.shape[1]
    vw = DN_HEADS * DN_DV
    sc_taps, dn_taps = sc_conv_w.shape[0], dn_conv_w.shape[0]

    o_scb, o_scc, o_sch, o_qkv = 0, sc, 2 * sc, 3 * sc
    o_z = o_qkv + qkv
    o_beta = o_z + vw
    o_a = o_beta + DN_HEADS
    o_gsc = o_a + DN_HEADS
    o_gdn = o_gsc + d
    seg = lambda o, w: w_in[:, o:o + w]
    w_main = jnp.concatenate([seg(o_qkv, qkv), seg(o_z, vw), seg(o_gsc, d), seg(o_gdn, d),
                              seg(o_scb, sc), seg(o_scc, sc), seg(o_sch, sc)], axis=1).astype(BF16)
    col_z, col_gsc, col_gdn, col_sc = qkv, qkv + vw, qkv + vw + d, qkv + vw + 2 * d
    w_ba = jnp.pad(jnp.concatenate([seg(o_beta, DN_HEADS), seg(o_a, DN_HEADS)], axis=1),
                   ((0, 0), (0, LANES - 2 * DN_HEADS))).astype(BF16)
    scw = jnp.pad(sc_conv_w, ((0, SUBLANES - sc_taps), (0, 0)))
    dnw = jnp.pad(dn_conv_w, ((0, SUBLANES - dn_taps), (0, 0)))
    alog = _lane_row(a_log, DN_HEADS)
    dtb = _lane_row(dt_bias, DN_HEADS)
    ng = dn_norm_g.reshape(1, DN_DV).astype(F32)
    wsc, wdn, wo = w_br_sc.astype(BF16), w_br_dn.astype(BF16), w_o.astype(BF16)
    wqt = peer_wq.T.astype(BF16)
    keys = peer_keys.reshape(2 * PEER_HEADS, PEER_NKEYS, PEER_HALF).astype(BF16)
    u_tab = peer_u.astype(BF16)
    vt_tab = peer_v.T.astype(BF16)
    g1 = norm1_g.reshape(1, d)
    g2 = norm2_g.reshape(1, d)
    gf = jnp.ones((1, d), F32) if final_g is None else final_g.reshape(1, d)

    outs, new_states = [], []
    for x3, (dn_state, dn_buf, sc_buf) in zip(hs, states):
        nb, t, _ = x3.shape
        n = nb * t
        x = x3.reshape(n, d)
        proj, ba = _inproj(x, g1, w_main, w_ba, _pick(n, 512), 1024)
        ysc, q, k, v, bx, gx, sc_new, dn_new = _prep(
            proj, ba, _pad_rows_front(sc_buf, SUBLANES), _pad_rows_front(dn_buf, SUBLANES), scw, dnw, alog, dtb,
            nb, t, _pick(t, 256), sc, qkv, col_sc, sc_taps, dn_taps)
        o, s_new = _delta(q, k, v, bx, gx, dn_state.astype(F32), nb, t)
        h1 = _merge(x, ysc, o, proj, ng, wsc, wdn, wo, _pick(n, 256), col_z, col_gsc, col_gdn)
        xnt, fa, fb, fc, fr = _router(h1, g2, wqt, keys, _pick(n, 256))
        y = _experts(xnt, u_tab, vt_tab, fa, fb, fc, fr, h1, gf, _pick(n, 512), 1024, final_g is not None)
        outs.append(y.reshape(nb, t, d))
        new_states.append((s_new.astype(dn_state.dtype), dn_new[:, SUBLANES - (dn_taps - 1):, :],
                           sc_new[:, SUBLANES - (sc_taps - 1):, :]))
    return outs, new_states


def kernel(x_prompt, x_sample, state_dn, state_dn_conv, state_sc_conv, norm1_g, w_in, sc_conv_w, dn_conv_w,
           dn_a_log, dn_dt_bias, dn_norm_g, w_br_sc, w_br_dn, w_o, norm2_g, peer_wq, peer_keys, peer_u, peer_v,
           final_norm_g):
    depth = w_in.shape[0]
    nbp = x_prompt.shape[0]
    hs = [x_prompt, x_sample]
    per_layer = []
    for l in range(depth):
        wts = (norm1_g[l], w_in[l], sc_conv_w[l], dn_conv_w[l], dn_a_log[l], dn_dt_bias[l], dn_norm_g[l],
               w_br_sc[l], w_br_dn[l], w_o[l], norm2_g[l], peer_wq[l], peer_keys[l], peer_u[l], peer_v[l])
        zero_states = (jnp.zeros((nbp, DN_HEADS, DN_DK, DN_DV), state_dn.dtype),
                       jnp.zeros((nbp, dn_conv_w.shape[1] - 1, dn_conv_w.shape[2]), x_prompt.dtype),
                       jnp.zeros((nbp, sc_conv_w.shape[1] - 1, sc_conv_w.shape[2]), x_prompt.dtype))
        states = [zero_states, (state_dn[l], state_dn_conv[l], state_sc_conv[l])]
        hs, new_states = _layer(hs, states, wts, final_norm_g if l == depth - 1 else None)
        per_layer.append(new_states)
    stack = lambda g, i: jnp.stack([per_layer[l][g][i] for l in range(depth)])
    return (hs[0], hs[1], stack(0, 0), stack(0, 1), stack(0, 2), stack(1, 0), stack(1, 1), stack(1, 2))
```

```python
import functools
import math

import jax
import jax.numpy as jnp
from jax import lax
from jax.experimental import pallas as pl
from jax.experimental.pallas import tpu as pltpu

F32 = jnp.float32
BF16 = jnp.bfloat16
EPS = 1e-6
LANES = 128
SUBLANES = 8
BF16_ROWS = 16
CHUNK = 64
DN_HEADS = 8
DN_DK = 128
DN_DV = 128
PEER_HEADS = 8
PEER_NKEYS = 128
PEER_HALF = 128
PEER_TOPK = 16
NEG_INF = float("-inf")
HIGHEST = lax.Precision.HIGHEST
MIB = 2 ** 20


def _params(sem, vmem_mib):
    return pltpu.CompilerParams(dimension_semantics=sem, vmem_limit_bytes=vmem_mib * MIB)


def _dot(a, b):
    return jnp.dot(a.astype(BF16), b.astype(BF16), preferred_element_type=F32)


def _dot_nt(a, b):
    return lax.dot_general(a.astype(BF16), b.astype(BF16), (((1,), (1,)), ((), ())),
                           preferred_element_type=F32)


def _dot_tn(a, b):
    return lax.dot_general(a.astype(BF16), b.astype(BF16), (((0,), (0,)), ((), ())),
                           preferred_element_type=F32)


def _dot_hi(a, b):
    return jnp.dot(a, b, preferred_element_type=F32, precision=HIGHEST)


def _split(x):
    hi = x.astype(BF16)
    return hi, (x - hi.astype(F32)).astype(BF16)


def _dot3(a, b):
    ah, al = _split(a)
    bh, bl = _split(b)
    return jnp.dot(jnp.concatenate([ah, al, ah], axis=1), jnp.concatenate([bh, bh, bl], axis=0),
                   preferred_element_type=F32)


def _rms(x, g):
    return x * lax.rsqrt(jnp.mean(x * x, axis=-1, keepdims=True) + EPS) * g


def _softplus(x):
    return jnp.maximum(x, 0.0) + jnp.log1p(jnp.exp(-jnp.abs(x)))


def _inproj_body(x_ref, g_ref, w_ref, wba_ref, proj_ref, ba_ref, xn_ref):
    @pl.when(pl.program_id(1) == 0)
    def _():
        xn = _rms(x_ref[...], g_ref[...]).astype(BF16)
        xn_ref[...] = xn
        ba_ref[...] = jnp.dot(xn, wba_ref[...], preferred_element_type=F32)

    proj_ref[...] = jnp.dot(xn_ref[...], w_ref[...], preferred_element_type=F32)


def _inproj(x, g, w_main, w_ba, tm, tn):
    n, d = x.shape
    width = w_main.shape[1]
    return pl.pallas_call(
        _inproj_body,
        grid=(n // tm, width // tn),
        in_specs=[
            pl.BlockSpec((tm, d), lambda i, j: (i, 0)),
            pl.BlockSpec((1, d), lambda i, j: (0, 0)),
            pl.BlockSpec((d, tn), lambda i, j: (0, j)),
            pl.BlockSpec((d, LANES), lambda i, j: (0, 0)),
        ],
        out_specs=[
            pl.BlockSpec((tm, tn), lambda i, j: (i, j)),
            pl.BlockSpec((tm, LANES), lambda i, j: (i, 0)),
        ],
        out_shape=[jax.ShapeDtypeStruct((n, width), F32), jax.ShapeDtypeStruct((n, LANES), F32)],
        scratch_shapes=[pltpu.VMEM((tm, d), BF16)],
        compiler_params=_params(("parallel", "arbitrary"), 48),
        name="inproj",
    )(x, g, w_main, w_ba)


def _prep_body(scb_ref, scc_ref, sch_ref, qkv_ref, ba_ref, scbuf_ref, dnbuf_ref, scw_ref, dnw_ref,
               alog_ref, dtb_ref,
               ysc_ref, q_ref, k_ref, v_ref, bx_ref, gx_ref, scnew_ref, dnnew_ref,
               ext_sc, ext_dn, *, sc_taps, dn_taps):
    tt = scb_ref.shape[0]
    halo = SUBLANES

    @pl.when(pl.program_id(1) == 0)
    def _():
        ext_sc[0:halo, :] = scbuf_ref[...]
        ext_dn[0:halo, :] = dnbuf_ref[...]

    ext_sc[halo:halo + tt, :] = scc_ref[...] * sch_ref[...]
    for cg in range(scb_ref.shape[1] // LANES):
        cs = slice(cg * LANES, (cg + 1) * LANES)
        acc = None
        for i in range(sc_taps):
            r0 = halo - (sc_taps - 1) + i
            term = ext_sc[r0:r0 + tt, cs] * scw_ref[i:i + 1, cs]
            acc = term if acc is None else acc + term
        ysc_ref[:, cs] = (scb_ref[:, cs] * acc).astype(BF16)
    scnew_ref[...] = ext_sc[tt:tt + halo, :]
    ext_sc[0:halo, :] = ext_sc[tt:tt + halo, :]

    ext_dn[halo:halo + tt, :] = qkv_ref[...]
    n_qk = DN_HEADS * DN_DK // LANES
    for cg in range(qkv_ref.shape[1] // LANES):
        cs = slice(cg * LANES, (cg + 1) * LANES)
        acc = None
        for i in range(dn_taps):
            r0 = halo - (dn_taps - 1) + i
            term = ext_dn[r0:r0 + tt, cs] * dnw_ref[i:i + 1, cs]
            acc = term if acc is None else acc + term
        a = acc * jax.nn.sigmoid(acc)
        if cg < 2 * n_qk:
            a = a * lax.rsqrt(jnp.sum(a * a, axis=-1, keepdims=True) + EPS)
        if cg < n_qk:
            q_ref[:, cs] = a * (DN_DK ** -0.5)
        elif cg < 2 * n_qk:
            k_ref[:, slice((cg - n_qk) * LANES, (cg - n_qk + 1) * LANES)] = a
        else:
            v_ref[:, slice((cg - 2 * n_qk) * LANES, (cg - 2 * n_qk + 1) * LANES)] = a
    dnnew_ref[...] = ext_dn[tt:tt + halo, :]
    ext_dn[0:halo, :] = ext_dn[tt:tt + halo, :]

    ba = ba_ref[...]
    beta = jax.nn.sigmoid(ba)
    g = -jnp.exp(alog_ref[...]) * _softplus(ba + dtb_ref[...])
    for h in range(DN_HEADS):
        cs = slice(h * LANES, (h + 1) * LANES)
        bx_ref[:, cs] = jnp.broadcast_to(beta[:, h:h + 1], (tt, LANES))
        gx_ref[:, cs] = jnp.broadcast_to(g[:, DN_HEADS + h:DN_HEADS + h + 1], (tt, LANES))


def _prep(proj, ba, scbuf, dnbuf, scw, dnw, alog, dtb, nb, t, tt, sc, qkv, col0_sc, sc_taps, dn_taps):
    n = nb * t
    nt = t // tt
    vw = DN_HEADS * DN_DV
    row = lambda b, s: b * nt + s
    scblk = col0_sc // sc
    body = functools.partial(_prep_body, sc_taps=sc_taps, dn_taps=dn_taps)
    return pl.pallas_call(
        body,
        grid=(nb, nt),
        in_specs=[
            pl.BlockSpec((tt, sc), lambda b, s: (row(b, s), scblk)),
            pl.BlockSpec((tt, sc), lambda b, s: (row(b, s), scblk + 1)),
            pl.BlockSpec((tt, sc), lambda b, s: (row(b, s), scblk + 2)),
            pl.BlockSpec((tt, qkv), lambda b, s: (row(b, s), 0)),
            pl.BlockSpec((tt, LANES), lambda b, s: (row(b, s), 0)),
            pl.BlockSpec((None, SUBLANES, sc), lambda b, s: (b, 0, 0)),
            pl.BlockSpec((None, SUBLANES, qkv), lambda b, s: (b, 0, 0)),
            pl.BlockSpec((SUBLANES, sc), lambda b, s: (0, 0)),
            pl.BlockSpec((SUBLANES, qkv), lambda b, s: (0, 0)),
            pl.BlockSpec((1, LANES), lambda b, s: (0, 0)),
            pl.BlockSpec((1, LANES), lambda b, s: (0, 0)),
        ],
        out_specs=[
            pl.BlockSpec((tt, sc), lambda b, s: (row(b, s), 0)),
            pl.BlockSpec((tt, vw), lambda b, s: (row(b, s), 0)),
            pl.BlockSpec((tt, vw), lambda b, s: (row(b, s), 0)),
            pl.BlockSpec((tt, vw), lambda b, s: (row(b, s), 0)),
            pl.BlockSpec((tt, vw), lambda b, s: (row(b, s), 0)),
            pl.BlockSpec((tt, vw), lambda b, s: (row(b, s), 0)),
            pl.BlockSpec((None, SUBLANES, sc), lambda b, s: (b, 0, 0)),
            pl.BlockSpec((None, SUBLANES, qkv), lambda b, s: (b, 0, 0)),
        ],
        out_shape=[
            jax.ShapeDtypeStruct((n, sc), BF16),
            jax.ShapeDtypeStruct((n, vw), F32),
            jax.ShapeDtypeStruct((n, vw), F32),
            jax.ShapeDtypeStruct((n, vw), F32),
            jax.ShapeDtypeStruct((n, vw), F32),
            jax.ShapeDtypeStruct((n, vw), F32),
            jax.ShapeDtypeStruct((nb, SUBLANES, sc), F32),
            jax.ShapeDtypeStruct((nb, SUBLANES, qkv), F32),
        ],
        scratch_shapes=[pltpu.VMEM((tt + SUBLANES, sc), F32), pltpu.VMEM((tt + SUBLANES, qkv), F32)],
        compiler_params=_params(("parallel", "arbitrary"), 48),
        name="prep",
    )(proj, proj, proj, proj, ba, scbuf, dnbuf, scw, dnw, alog, dtb)


_INV_BASE = 8

def _delta_body(q_ref, k_ref, v_ref, bx_ref, gx_ref, s0_ref, o_ref, sout_ref, s_ref, low_ref, qk_ref):
    c = pl.program_id(1)
    nc = pl.num_programs(1)

    @pl.when(c == 0)
    def _():
        s_ref[...] = s0_ref[...]

    ri = lax.broadcasted_iota(jnp.int32, (CHUNK, CHUNK), 0)
    ci = lax.broadcasted_iota(jnp.int32, (CHUNK, CHUNK), 1)
    causal = ri >= ci
    strict = ri > ci
    eye = (ri == ci).astype(F32)
    base = _INV_BASE
    base_blocks = (ri // base) == (ci // base)
    levels = []
    s = base
    while s < CHUNK:
        levels.append(((ri // (2 * s)) == (ci // (2 * s))) & ((ri // s) % 2 == 1) & ((ci // s) % 2 == 0))
        s *= 2

    gx = gx_ref[...]
    rw = lax.broadcasted_iota(jnp.int32, (CHUNK, LANES), 0)
    cw = lax.broadcasted_iota(jnp.int32, (CHUNK, LANES), 1)
    upper = jnp.concatenate([(rw <= cw).astype(F32)] * DN_HEADS, axis=1)
    gc_all = _dot_hi(causal.astype(F32), gx)
    gcrow_all = _dot_hi(jnp.ones((CHUNK, CHUNK), F32), gx * upper)
    heads = range(DN_HEADS)
    cs = [slice(h * LANES, (h + 1) * LANES) for h in heads]
    kk = [_dot_nt(jnp.concatenate([k_ref[:, cs[h]] * bx_ref[:, cs[h]], q_ref[:, cs[h]]], axis=0), k_ref[:, cs[h]])
          for h in heads]
    for h in heads:
        decay = jnp.exp(jnp.where(causal, gc_all[:, cs[h]][:, :CHUNK] - gcrow_all[:, cs[h]][:, :CHUNK], NEG_INF))
        low_ref[h] = jnp.where(strict, kk[h][:CHUNK] * decay, 0.0)
        qk_ref[h] = (kk[h][CHUNK:] * decay).astype(BF16)
    m = [jnp.where(base_blocks, -low_ref[h], 0.0) for h in heads]
    tmat = [eye + m[h] for h in heads]
    p = 1
    while 2 * p < base:
        m = [_dot3(m[h], m[h]) for h in heads]
        tm = [_dot3(tmat[h], m[h]) for h in heads]
        tmat = [tmat[h] + tm[h] for h in heads]
        p *= 2
    for lvl in levels:
        tc = [_dot3(tmat[h], jnp.where(lvl, low_ref[h], 0.0)) for h in heads]
        tct = [_dot3(tc[h], tmat[h]) for h in heads]
        tmat = [tmat[h] - tct[h] for h in heads]
    uw = []
    for h in heads:
        k = k_ref[:, cs[h]]
        bx = bx_ref[:, cs[h]]
        rhs = jnp.concatenate([v_ref[:, cs[h]] * bx, k * bx * jnp.exp(gc_all[:, cs[h]])], axis=1)
        uw.append(_dot(tmat[h], rhs))
    ws_qs = [_dot(jnp.concatenate([uw[h][:, DN_DV:], q_ref[:, cs[h]] * jnp.exp(gc_all[:, cs[h]])], axis=0), s_ref[h])
             for h in heads]
    v_new = [uw[h][:, :DN_DV] - ws_qs[h][:CHUNK] for h in heads]
    qv = [jnp.dot(qk_ref[h], v_new[h].astype(BF16), preferred_element_type=F32) for h in heads]
    for h in heads:
        o_ref[:, cs[h]] = ws_qs[h][CHUNK:] + qv[h]
    for h in heads:
        gc = gc_all[:, cs[h]]
        g_last = gc[CHUNK - 1:CHUNK, :]
        s_ref[h] = s_ref[h] * jnp.exp(g_last) + _dot_tn(k_ref[:, cs[h]] * jnp.exp(g_last - gc), v_new[h])

    @pl.when(c == nc - 1)
    def _():
        sout_ref[...] = s_ref[...]


def _delta(q, k, v, bx, gx, s0, nb, t):
    n = nb * t
    nc = t // CHUNK
    vw = DN_HEADS * DN_DV
    tok = pl.BlockSpec((CHUNK, vw), lambda b, c: (b * nc + c, 0))
    st = pl.BlockSpec((None, DN_HEADS, DN_DK, DN_DV), lambda b, c: (b, 0, 0, 0))
    return pl.pallas_call(
        _delta_body,
        grid=(nb, nc),
        in_specs=[tok, tok, tok, tok, tok, st],
        out_specs=[tok, st],
        out_shape=[jax.ShapeDtypeStruct((n, vw), F32),
                   jax.ShapeDtypeStruct((nb, DN_HEADS, DN_DK, DN_DV), F32)],
        scratch_shapes=[pltpu.VMEM((DN_HEADS, DN_DK, DN_DV), F32),
                        pltpu.VMEM((DN_HEADS, CHUNK, CHUNK), F32),
                        pltpu.VMEM((DN_HEADS, CHUNK, CHUNK), BF16)],
        compiler_params=_params(("parallel", "arbitrary"), 32),
        name="delta",
    )(q, k, v, bx, gx, s0)


def _merge_body(x_ref, ysc_ref, o_ref, z_ref, gsc_ref, gdn_ref, ng_ref, wsc_ref, wdn_ref, wo_ref, h_ref):
    tm = x_ref.shape[0]
    parts = []
    for h in range(DN_HEADS):
        cs = slice(h * LANES, (h + 1) * LANES)
        z = z_ref[:, cs]
        on = _rms(o_ref[:, cs], ng_ref[...]) * (z * jax.nn.sigmoid(z))
        parts.append(on.astype(BF16))
    y_dn = jnp.dot(jnp.concatenate(parts, axis=1), wdn_ref[...], preferred_element_type=F32)
    y_sc = jnp.dot(ysc_ref[...], wsc_ref[...], preferred_element_type=F32)
    merged = jax.nn.sigmoid(gsc_ref[...]) * y_sc + jax.nn.sigmoid(gdn_ref[...]) * y_dn
    h_ref[...] = x_ref[...] + jnp.dot(merged.astype(BF16), wo_ref[...], preferred_element_type=F32)


def _merge(x, ysc, o, proj, ng, wsc, wdn, wo, tm, col0_z, col0_gsc, col0_gdn):
    n, d = x.shape
    sc = ysc.shape[1]
    vw = o.shape[1]
    const = lambda shape: pl.BlockSpec(shape, lambda i: (0, 0), pipeline_mode=pl.Buffered(1))
    return pl.pallas_call(
        _merge_body,
        grid=(n // tm,),
        in_specs=[
            pl.BlockSpec((tm, d), lambda i: (i, 0)),
            pl.BlockSpec((tm, sc), lambda i: (i, 0)),
            pl.BlockSpec((tm, vw), lambda i: (i, 0)),
            pl.BlockSpec((tm, vw), lambda i: (i, col0_z // vw)),
            pl.BlockSpec((tm, d), lambda i: (i, col0_gsc // d)),
            pl.BlockSpec((tm, d), lambda i: (i, col0_gdn // d)),
            const((1, LANES)),
            const((sc, d)),
            const((vw, d)),
            const((d, d)),
        ],
        out_specs=pl.BlockSpec((tm, d), lambda i: (i, 0)),
        out_shape=jax.ShapeDtypeStruct((n, d), F32),
        compiler_params=_params(("parallel",), 48),
        name="merge",
    )(x, ysc, o, proj, proj, proj, ng, wsc, wdn, wo)


_CAND_ROWS = 10 * SUBLANES


def _tree(op, xs):
    xs = list(xs)
    while len(xs) > 1:
        xs = [op(xs[i], xs[i + 1]) for i in range(0, len(xs) - 1, 2)] + (xs[-1:] if len(xs) % 2 else [])
    return xs[0]


_NO_INDEX = float(2 ** 20)


def _topk_ranks(scores):
    nk, ln = scores[0].shape
    ng = nk // SUBLANES
    probs = range(len(scores))
    sub = lax.broadcasted_iota(jnp.int32, (SUBLANES, ln), 0).astype(F32)
    vals = [[s[g * SUBLANES:(g + 1) * SUBLANES, :] for g in range(ng)] for s in scores]
    rank = [[jnp.full((SUBLANES, ln), float(PEER_TOPK), F32) for _ in range(ng)] for _ in probs]
    base = [float(g * SUBLANES) for g in range(ng)]
    tops = [[] for _ in probs]
    for it in range(PEER_TOPK):
        m8 = [_tree(jnp.maximum, vals[p]) for p in probs]
        m = [jnp.max(m8[p], axis=0, keepdims=True) for p in probs]
        cand = [_tree(jnp.minimum, [jnp.where(vals[p][g] == m[p], base[g], _NO_INDEX) for g in range(ng)])
                for p in probs]
        first = [jnp.min(cand[p] + sub, axis=0, keepdims=True) for p in probs]
        grp = [first[p] - sub for p in probs]
        for p in probs:
            tops[p].append(m[p])
            for g in range(ng):
                sel = grp[p] == base[g]
                rank[p][g] = jnp.where(sel, float(it), rank[p][g])
                vals[p][g] = jnp.where(sel, NEG_INF, vals[p][g])
    return [(jnp.concatenate(rank[p], axis=0), jnp.concatenate(tops[p], axis=0)) for p in probs]


def _joint_select(pairs):
    ln = pairs[0][0].shape[1]
    probs = range(len(pairs))
    sub = lax.broadcasted_iota(jnp.int32, (SUBLANES, ln), 0).astype(F32)
    isub = lax.broadcasted_iota(jnp.int32, (SUBLANES, ln), 0)
    pos_of = [r * PEER_TOPK + sub for r in range(SUBLANES)]
    pos_of.append(SUBLANES + sub)
    pos_of.append((SUBLANES + sub) * PEER_TOPK)
    ngr = len(pos_of)
    vals = []
    for v1, v2 in pairs:
        blocks = []
        for r in range(SUBLANES):
            cnt = min(SUBLANES, PEER_TOPK // (r + 1))
            blocks.append(jnp.where(isub < cnt, v1[r:r + 1, :] + v2[0:SUBLANES, :], NEG_INF))
        blocks.append(v1[0:1, :] + v2[SUBLANES:, :])
        blocks.append(v1[SUBLANES:, :] + v2[0:1, :])
        vals.append(blocks)
    chosen = [[jnp.zeros((SUBLANES, ln), F32) for _ in range(ngr)] for _ in probs]
    for _ in range(PEER_TOPK):
        m8 = [_tree(jnp.maximum, vals[p]) for p in probs]
        m = [jnp.max(m8[p], axis=0, keepdims=True) for p in probs]
        pos = [[jnp.where(vals[p][g] == m[p], pos_of[g], _NO_INDEX) for g in range(ngr)] for p in probs]
        first = [jnp.min(_tree(jnp.minimum, pos[p]), axis=0, keepdims=True) for p in probs]
        for p in probs:
            for g in range(ngr):
                sel = pos[p][g] == first[p]
                chosen[p][g] = jnp.where(sel, 1.0, chosen[p][g])
                vals[p][g] = jnp.where(sel, NEG_INF, vals[p][g])
    return [jnp.concatenate(chosen[p], axis=0) for p in probs]


def _router_body(h_ref, g_ref, wqt_ref, keys_ref, xnt_ref, a_ref, b_ref, c_ref, r_ref, qt_ref, st_ref, rk_ref, vl_ref):
    tm = h_ref.shape[0]
    xn = _rms(h_ref[...], g_ref[...])
    xnt = xn.T.astype(BF16)
    xnt_ref[...] = xnt
    qt_ref[...] = jnp.dot(wqt_ref[...], xnt, preferred_element_type=F32).astype(BF16)

    def score_step(hp, carry):
        q = qt_ref[pl.ds(pl.multiple_of(hp * PEER_HALF, PEER_HALF), PEER_HALF), :]
        st_ref[hp] = jnp.dot(keys_ref[hp], q, preferred_element_type=F32)
        return carry

    lax.fori_loop(0, 2 * PEER_HEADS, score_step, 0)

    chunks = [slice(lc * LANES, (lc + 1) * LANES) for lc in range(tm // LANES)]

    def rank_step(hp, carry):
        for ls, (rank, vals) in zip(chunks, _topk_ranks([st_ref[hp, :, ls] for ls in chunks])):
            rk_ref[hp, :, ls] = rank
            vl_ref[hp, :, ls] = vals
        return carry

    lax.fori_loop(0, 2 * PEER_HEADS, rank_step, 0)

    def head_step(h, carry):
        tops = [(vl_ref[2 * h, :, ls], vl_ref[2 * h + 1, :, ls]) for ls in chunks]
        for ls, (v1, v2), chosen in zip(chunks, tops, _joint_select(tops)):
            e1 = jnp.exp(v1 - v1[0:1, :])
            e2 = jnp.exp(v2 - v2[0:1, :])
            z = jnp.zeros((1, LANES), F32)
            c_rows = []
            for r in range(SUBLANES):
                blk = chosen[r * SUBLANES:(r + 1) * SUBLANES, :]
                cnt = jnp.sum(blk, axis=0, keepdims=True)
                z = z + e1[r:r + 1, :] * jnp.sum(blk * e2[0:SUBLANES, :], axis=0, keepdims=True)
                if r == 0:
                    strip = chosen[SUBLANES * SUBLANES:SUBLANES * SUBLANES + SUBLANES, :]
                    cnt = cnt + jnp.sum(strip, axis=0, keepdims=True)
                    z = z + e1[0:1, :] * jnp.sum(strip * e2[SUBLANES:, :], axis=0, keepdims=True)
                c_rows.append(cnt)
            strip = chosen[SUBLANES * SUBLANES + SUBLANES:, :]
            z = z + e2[0:1, :] * jnp.sum(strip * e1[SUBLANES:, :], axis=0, keepdims=True)
            counts = jnp.concatenate(c_rows + [strip], axis=0)
            inv_z = 1.0 / z

            rank1 = rk_ref[2 * h, :, ls]
            rank2 = rk_ref[2 * h + 1, :, ls]
            in1 = rank1 < float(PEER_TOPK)
            in2 = rank2 < float(PEER_TOPK)
            a_key = jnp.exp(jnp.where(in1, st_ref[2 * h, :, ls] - v1[0:1, :], NEG_INF)) * inv_z
            b_key = jnp.exp(jnp.where(in2, st_ref[2 * h + 1, :, ls] - v2[0:1, :], NEG_INF))
            c_key = jnp.zeros((PEER_NKEYS, LANES), F32)
            for r in range(PEER_TOPK):
                c_key = jnp.where(rank1 == float(r), counts[r:r + 1, :], c_key)
            a_ref[h, :, ls] = a_key
            c_ref[h, :, ls] = c_key
            b_ref[h, :, ls] = b_key.astype(BF16)
            r_ref[h, :, ls] = rank2.astype(BF16)
        return carry

    lax.fori_loop(0, PEER_HEADS, head_step, 0)


def _router(h, g, wqt, keys, tm):
    n, d = h.shape
    qd = wqt.shape[0]
    n_hp = 2 * PEER_HEADS
    fac = lambda: pl.BlockSpec((PEER_HEADS, PEER_NKEYS, tm), lambda i: (0, 0, i))
    fac_shape = lambda dt: jax.ShapeDtypeStruct((PEER_HEADS, PEER_NKEYS, n), dt)
    return pl.pallas_call(
        _router_body,
        grid=(n // tm,),
        in_specs=[
            pl.BlockSpec((tm, d), lambda i: (i, 0)),
            pl.BlockSpec((1, d), lambda i: (0, 0)),
            pl.BlockSpec((qd, d), lambda i: (0, 0), pipeline_mode=pl.Buffered(1)),
            pl.BlockSpec((n_hp, PEER_NKEYS, PEER_HALF), lambda i: (0, 0, 0), pipeline_mode=pl.Buffered(1)),
        ],
        out_specs=[pl.BlockSpec((d, tm), lambda i: (0, i)), fac(), fac(), fac(), fac()],
        out_shape=[jax.ShapeDtypeStruct((d, n), BF16), fac_shape(F32), fac_shape(BF16), fac_shape(F32),
                   fac_shape(BF16)],
        scratch_shapes=[
            pltpu.VMEM((qd, tm), BF16),
            pltpu.VMEM((n_hp, PEER_NKEYS, tm), F32),
            pltpu.VMEM((n_hp, PEER_NKEYS, tm), F32),
            pltpu.VMEM((n_hp, PEER_TOPK, tm), F32),
        ],
        compiler_params=_params(("parallel",), 48),
        name="router",
    )(h, g, wqt, keys)


def _experts_body(xnt_ref, u_ref, vt_ref, a_ref, b_ref, c_ref, r_ref, h_ref, g_ref, y_ref, acc_ref, w_ref, *,
                  final_norm):
    e = pl.program_id(1)
    n_i = u_ref.shape[0] // PEER_NKEYS
    tm = xnt_ref.shape[1]

    @pl.when(e == 0)
    def _():
        acc_ref[...] = jnp.zeros_like(acc_ref)

    hid = jnp.dot(u_ref[...], xnt_ref[...], preferred_element_type=F32)
    pk = b_ref.shape[2]
    for ii in range(n_i):
        gate = jnp.zeros((PEER_NKEYS // pk, pk, tm), BF16)
        for h in range(PEER_HEADS):
            a_i = jnp.broadcast_to(a_ref[h, ii:ii + 1, :], (pk, tm)).astype(BF16)
            c_i = jnp.broadcast_to(c_ref[h, ii:ii + 1, :], (pk, tm)).astype(BF16)
            gate = gate + jnp.where(r_ref[h] < c_i, a_i * b_ref[h], jnp.zeros((), BF16))
        x = hid[ii * PEER_NKEYS:(ii + 1) * PEER_NKEYS, :]
        act = 0.5 * x * (1.0 + lax.erf(x * (2.0 ** -0.5)))
        w_ref[ii * PEER_NKEYS:(ii + 1) * PEER_NKEYS, :] = act.astype(BF16) * gate.reshape(PEER_NKEYS, tm)
    acc_ref[...] += jnp.dot(vt_ref[...], w_ref[...], preferred_element_type=F32)

    @pl.when(e == pl.num_programs(1) - 1)
    def _():
        y = h_ref[...] + acc_ref[...].T
        y_ref[...] = _rms(y, g_ref[...]) if final_norm else y


def _experts(xnt, u, vt, a, b, c, r, h, g, tm, te, final_norm):
    d, n = xnt.shape
    n_i = te // PEER_NKEYS
    pk = b.shape[2]
    full = lambda: pl.BlockSpec((PEER_HEADS, PEER_NKEYS // pk, pk, tm), lambda i, e: (0, 0, 0, i))
    rows = lambda: pl.BlockSpec((PEER_HEADS, n_i, tm), lambda i, e: (0, e, i))
    return pl.pallas_call(
        functools.partial(_experts_body, final_norm=final_norm),
        grid=(n // tm, u.shape[0] // te),
        in_specs=[
            pl.BlockSpec((d, tm), lambda i, e: (0, i)),
            pl.BlockSpec((te, d), lambda i, e: (e, 0)),
            pl.BlockSpec((d, te), lambda i, e: (0, e)),
            rows(), full(), rows(), full(),
            pl.BlockSpec((tm, d), lambda i, e: (i, 0)),
            pl.BlockSpec((1, d), lambda i, e: (0, 0)),
        ],
        out_specs=pl.BlockSpec((tm, d), lambda i, e: (i, 0)),
        out_shape=jax.ShapeDtypeStruct((n, d), F32),
        scratch_shapes=[pltpu.VMEM((d, tm), F32), pltpu.VMEM((te, tm), BF16)],
        compiler_params=_params(("parallel", "arbitrary"), 56),
        name="experts",
    )(xnt, u, vt, a, b, c, r, h, g)


def _pick(n, pref):
    t = min(pref, n)
    while n % t:
        t //= 2
    return t


def _pad_rows_front(a, rows):
    return jnp.pad(a, ((0, 0), (rows - a.shape[1], 0), (0, 0)))


def _lane_row(vals, offset):
    return jnp.zeros((1, LANES), F32).at[0, offset:offset + vals.shape[0]].set(vals.astype(F32))


def _layer(hs, states, wts, final_g):
    (norm1_g, w_in, sc_conv_w, dn_conv_w, a_log, dt_bias, dn_norm_g, w_br_sc, w_br_dn, w_o, norm2_g,
     peer_wq, peer_keys, peer_u, peer_v) = wts
    d = w_in.shape[0]
    sc = sc_conv_w.shape[1]
    qkv = dn_conv_w.shape[1]
    vw = DN_HEADS * DN_DV
    sc_taps, dn_taps = sc_conv_w.shape[0], dn_conv_w.shape[0]

    o_scb, o_scc, o_sch, o_qkv = 0, sc, 2 * sc, 3 * sc
    o_z = o_qkv + qkv
    o_beta = o_z + vw
    o_a = o_beta + DN_HEADS
    o_gsc = o_a + DN_HEADS
    o_gdn = o_gsc + d
    seg = lambda o, w: w_in[:, o:o + w]
    w_main = jnp.concatenate([seg(o_qkv, qkv), seg(o_z, vw), seg(o_gsc, d), seg(o_gdn, d),
                              seg(o_scb, sc), seg(o_scc, sc), seg(o_sch, sc)], axis=1).astype(BF16)
    col_z, col_gsc, col_gdn, col_sc = qkv, qkv + vw, qkv + vw + d, qkv + vw + 2 * d
    w_ba = jnp.pad(jnp.concatenate([seg(o_beta, DN_HEADS), seg(o_a, DN_HEADS)], axis=1),
                   ((0, 0), (0, LANES - 2 * DN_HEADS))).astype(BF16)
    scw = jnp.pad(sc_conv_w, ((0, SUBLANES - sc_taps), (0, 0)))
    dnw = jnp.pad(dn_conv_w, ((0, SUBLANES - dn_taps), (0, 0)))
    alog = _lane_row(a_log, DN_HEADS)
    dtb = _lane_row(dt_bias, DN_HEADS)
    ng = dn_norm_g.reshape(1, DN_DV).astype(F32)
    wsc, wdn, wo = w_br_sc.astype(BF16), w_br_dn.astype(BF16), w_o.astype(BF16)
    wqt = peer_wq.T.astype(BF16)
    keys = peer_keys.reshape(2 * PEER_HEADS, PEER_NKEYS, PEER_HALF).astype(BF16)
    u_tab = peer_u.astype(BF16)
    vt_tab = peer_v.T.astype(BF16)
    g1 = norm1_g.reshape(1, d)
    g2 = norm2_g.reshape(1, d)
    gf = jnp.ones((1, d), F32) if final_g is None else final_g.reshape(1, d)

    outs, new_states = [], []
    for x3, (dn_state, dn_buf, sc_buf) in zip(hs, states):
        nb, t, _ = x3.shape
        n = nb * t
        x = x3.reshape(n, d)
        proj, ba = _inproj(x, g1, w_main, w_ba, _pick(n, 512), 1024)
        ysc, q, k, v, bx, gx, sc_new, dn_new = _prep(
            proj, ba, _pad_rows_front(sc_buf, SUBLANES), _pad_rows_front(dn_buf, SUBLANES), scw, dnw, alog, dtb,
            nb, t, _pick(t, 256), sc, qkv, col_sc, sc_taps, dn_taps)
        o, s_new = _delta(q, k, v, bx, gx, dn_state.astype(F32), nb, t)
        h1 = _merge(x, ysc, o, proj, ng, wsc, wdn, wo, _pick(n, 256), col_z, col_gsc, col_gdn)
        xnt, fa, fb, fc, fr = _router(h1, g2, wqt, keys, _pick(n, 256))
        pack = lambda f: f.reshape(PEER_HEADS, PEER_NKEYS // BF16_ROWS, BF16_ROWS, n)
        y = _experts(xnt, u_tab, vt_tab, fa, pack(fb), fc, pack(fr), h1, gf, _pick(n, 512), 1024,
                     final_g is not None)
        outs.append(y.reshape(nb, t, d))
        new_states.append((s_new.astype(dn_state.dtype), dn_new[:, SUBLANES - (dn_taps - 1):, :],
                           sc_new[:, SUBLANES - (sc_taps - 1):, :]))
    return outs, new_states


def kernel(x_prompt, x_sample, state_dn, state_dn_conv, state_sc_conv, norm1_g, w_in, sc_conv_w, dn_conv_w,
           dn_a_log, dn_dt_bias, dn_norm_g, w_br_sc, w_br_dn, w_o, norm2_g, peer_wq, peer_keys, peer_u, peer_v,
           final_norm_g):
    depth = w_in.shape[0]
    nbp = x_prompt.shape[0]
    hs = [x_prompt, x_sample]
    per_layer = []
    for l in range(depth):
        wts = (norm1_g[l], w_in[l], sc_conv_w[l], dn_conv_w[l], dn_a_log[l], dn_dt_bias[l], dn_norm_g[l],
               w_br_sc[l], w_br_dn[l], w_o[l], norm2_g[l], peer_wq[l], peer_keys[l], peer_u[l], peer_v[l])
        zero_states = (jnp.zeros((nbp, DN_HEADS, DN_DK, DN_DV), state_dn.dtype),
                       jnp.zeros((nbp, dn_conv_w.shape[1] - 1, dn_conv_w.shape[2]), x_prompt.dtype),
                       jnp.zeros((nbp, sc_conv_w.shape[1] - 1, sc_conv_w.shape[2]), x_prompt.dtype))
        states = [zero_states, (state_dn[l], state_dn_conv[l], state_sc_conv[l])]
        hs, new_states = _layer(hs, states, wts, final_norm_g if l == depth - 1 else None)
        per_layer.append(new_states)
    stack = lambda g, i: jnp.stack([per_layer[l][g][i] for l in range(depth)])
    return (hs[0], hs[1], stack(0, 0), stack(0, 1), stack(0, 2), stack(1, 0), stack(1, 1), stack(1, 2))
```

```python
import functools
import math

import jax
import jax.numpy as jnp
from jax import lax
from jax.experimental import pallas as pl
from jax.experimental.pallas import tpu as pltpu

F32 = jnp.float32
BF16 = jnp.bfloat16
EPS = 1e-6
LANES = 128
SUBLANES = 8
BF16_ROWS = 16
GATE_STEP = 2.0
CHUNK = 64
DN_HEADS = 8
DN_DK = 128
DN_DV = 128
PEER_HEADS = 8
PEER_NKEYS = 128
PEER_HALF = 128
PEER_TOPK = 16
NEG_INF = float("-inf")
HIGHEST = lax.Precision.HIGHEST
MIB = 2 ** 20


def _params(sem, vmem_mib):
    return pltpu.CompilerParams(dimension_semantics=sem, vmem_limit_bytes=vmem_mib * MIB)


def _dot(a, b):
    return jnp.dot(a.astype(BF16), b.astype(BF16), preferred_element_type=F32)


def _dot_nt(a, b):
    return lax.dot_general(a.astype(BF16), b.astype(BF16), (((1,), (1,)), ((), ())),
                           preferred_element_type=F32)


def _dot_tn(a, b):
    return lax.dot_general(a.astype(BF16), b.astype(BF16), (((0,), (0,)), ((), ())),
                           preferred_element_type=F32)


def _dot_hi(a, b):
    return jnp.dot(a, b, preferred_element_type=F32, precision=HIGHEST)


def _split(x):
    hi = x.astype(BF16)
    return hi, (x - hi.astype(F32)).astype(BF16)


def _dot3(a, b):
    ah, al = _split(a)
    bh, bl = _split(b)
    return jnp.dot(jnp.concatenate([ah, al, ah], axis=1), jnp.concatenate([bh, bh, bl], axis=0),
                   preferred_element_type=F32)


def _rms(x, g):
    return x * lax.rsqrt(jnp.mean(x * x, axis=-1, keepdims=True) + EPS) * g


def _softplus(x):
    return jnp.maximum(x, 0.0) + jnp.log1p(jnp.exp(-jnp.abs(x)))


def _inproj_body(x_ref, g_ref, w_ref, wba_ref, proj_ref, ba_ref, xn_ref):
    @pl.when(pl.program_id(1) == 0)
    def _():
        xn = _rms(x_ref[...], g_ref[...]).astype(BF16)
        xn_ref[...] = xn
        ba_ref[...] = jnp.dot(xn, wba_ref[...], preferred_element_type=F32)

    proj_ref[...] = jnp.dot(xn_ref[...], w_ref[...], preferred_element_type=F32)


def _inproj(x, g, w_main, w_ba, tm, tn):
    n, d = x.shape
    width = w_main.shape[1]
    return pl.pallas_call(
        _inproj_body,
        grid=(n // tm, width // tn),
        in_specs=[
            pl.BlockSpec((tm, d), lambda i, j: (i, 0)),
            pl.BlockSpec((1, d), lambda i, j: (0, 0)),
            pl.BlockSpec((d, tn), lambda i, j: (0, j)),
            pl.BlockSpec((d, LANES), lambda i, j: (0, 0)),
        ],
        out_specs=[
            pl.BlockSpec((tm, tn), lambda i, j: (i, j)),
            pl.BlockSpec((tm, LANES), lambda i, j: (i, 0)),
        ],
        out_shape=[jax.ShapeDtypeStruct((n, width), F32), jax.ShapeDtypeStruct((n, LANES), F32)],
        scratch_shapes=[pltpu.VMEM((tm, d), BF16)],
        compiler_params=_params(("parallel", "arbitrary"), 48),
        name="inproj",
    )(x, g, w_main, w_ba)


def _prep_body(scb_ref, scc_ref, sch_ref, qkv_ref, ba_ref, scbuf_ref, dnbuf_ref, scw_ref, dnw_ref,
               alog_ref, dtb_ref,
               ysc_ref, q_ref, k_ref, v_ref, bx_ref, gx_ref, scnew_ref, dnnew_ref,
               ext_sc, ext_dn, *, sc_taps, dn_taps):
    tt = scb_ref.shape[0]
    halo = SUBLANES

    @pl.when(pl.program_id(1) == 0)
    def _():
        ext_sc[0:halo, :] = scbuf_ref[...]
        ext_dn[0:halo, :] = dnbuf_ref[...]

    ext_sc[halo:halo + tt, :] = scc_ref[...] * sch_ref[...]
    for cg in range(scb_ref.shape[1] // LANES):
        cs = slice(cg * LANES, (cg + 1) * LANES)
        acc = None
        for i in range(sc_taps):
            r0 = halo - (sc_taps - 1) + i
            term = ext_sc[r0:r0 + tt, cs] * scw_ref[i:i + 1, cs]
            acc = term if acc is None else acc + term
        ysc_ref[:, cs] = (scb_ref[:, cs] * acc).astype(BF16)
    scnew_ref[...] = ext_sc[tt:tt + halo, :]
    ext_sc[0:halo, :] = ext_sc[tt:tt + halo, :]

    ext_dn[halo:halo + tt, :] = qkv_ref[...]
    n_qk = DN_HEADS * DN_DK // LANES
    for cg in range(qkv_ref.shape[1] // LANES):
        cs = slice(cg * LANES, (cg + 1) * LANES)
        acc = None
        for i in range(dn_taps):
            r0 = halo - (dn_taps - 1) + i
            term = ext_dn[r0:r0 + tt, cs] * dnw_ref[i:i + 1, cs]
            acc = term if acc is None else acc + term
        a = acc * jax.nn.sigmoid(acc)
        if cg < 2 * n_qk:
            a = a * lax.rsqrt(jnp.sum(a * a, axis=-1, keepdims=True) + EPS)
        if cg < n_qk:
            q_ref[:, cs] = a * (DN_DK ** -0.5)
        elif cg < 2 * n_qk:
            k_ref[:, slice((cg - n_qk) * LANES, (cg - n_qk + 1) * LANES)] = a
        else:
            v_ref[:, slice((cg - 2 * n_qk) * LANES, (cg - 2 * n_qk + 1) * LANES)] = a
    dnnew_ref[...] = ext_dn[tt:tt + halo, :]
    ext_dn[0:halo, :] = ext_dn[tt:tt + halo, :]

    ba = ba_ref[...]
    beta = jax.nn.sigmoid(ba)
    g = -jnp.exp(alog_ref[...]) * _softplus(ba + dtb_ref[...])
    for h in range(DN_HEADS):
        cs = slice(h * LANES, (h + 1) * LANES)
        bx_ref[:, cs] = jnp.broadcast_to(beta[:, h:h + 1], (tt, LANES))
        gx_ref[:, cs] = jnp.broadcast_to(g[:, DN_HEADS + h:DN_HEADS + h + 1], (tt, LANES))


def _prep(proj, ba, scbuf, dnbuf, scw, dnw, alog, dtb, nb, t, tt, sc, qkv, col0_sc, sc_taps, dn_taps):
    n = nb * t
    nt = t // tt
    vw = DN_HEADS * DN_DV
    row = lambda b, s: b * nt + s
    scblk = col0_sc // sc
    body = functools.partial(_prep_body, sc_taps=sc_taps, dn_taps=dn_taps)
    return pl.pallas_call(
        body,
        grid=(nb, nt),
        in_specs=[
            pl.BlockSpec((tt, sc), lambda b, s: (row(b, s), scblk)),
            pl.BlockSpec((tt, sc), lambda b, s: (row(b, s), scblk + 1)),
            pl.BlockSpec((tt, sc), lambda b, s: (row(b, s), scblk + 2)),
            pl.BlockSpec((tt, qkv), lambda b, s: (row(b, s), 0)),
            pl.BlockSpec((tt, LANES), lambda b, s: (row(b, s), 0)),
            pl.BlockSpec((None, SUBLANES, sc), lambda b, s: (b, 0, 0)),
            pl.BlockSpec((None, SUBLANES, qkv), lambda b, s: (b, 0, 0)),
            pl.BlockSpec((SUBLANES, sc), lambda b, s: (0, 0)),
            pl.BlockSpec((SUBLANES, qkv), lambda b, s: (0, 0)),
            pl.BlockSpec((1, LANES), lambda b, s: (0, 0)),
            pl.BlockSpec((1, LANES), lambda b, s: (0, 0)),
        ],
        out_specs=[
            pl.BlockSpec((tt, sc), lambda b, s: (row(b, s), 0)),
            pl.BlockSpec((tt, vw), lambda b, s: (row(b, s), 0)),
            pl.BlockSpec((tt, vw), lambda b, s: (row(b, s), 0)),
            pl.BlockSpec((tt, vw), lambda b, s: (row(b, s), 0)),
            pl.BlockSpec((tt, vw), lambda b, s: (row(b, s), 0)),
            pl.BlockSpec((tt, vw), lambda b, s: (row(b, s), 0)),
            pl.BlockSpec((None, SUBLANES, sc), lambda b, s: (b, 0, 0)),
            pl.BlockSpec((None, SUBLANES, qkv), lambda b, s: (b, 0, 0)),
        ],
        out_shape=[
            jax.ShapeDtypeStruct((n, sc), BF16),
            jax.ShapeDtypeStruct((n, vw), F32),
            jax.ShapeDtypeStruct((n, vw), F32),
            jax.ShapeDtypeStruct((n, vw), F32),
            jax.ShapeDtypeStruct((n, vw), F32),
            jax.ShapeDtypeStruct((n, vw), F32),
            jax.ShapeDtypeStruct((nb, SUBLANES, sc), F32),
            jax.ShapeDtypeStruct((nb, SUBLANES, qkv), F32),
        ],
        scratch_shapes=[pltpu.VMEM((tt + SUBLANES, sc), F32), pltpu.VMEM((tt + SUBLANES, qkv), F32)],
        compiler_params=_params(("parallel", "arbitrary"), 48),
        name="prep",
    )(proj, proj, proj, proj, ba, scbuf, dnbuf, scw, dnw, alog, dtb)


_INV_BASE = 8

def _delta_body(q_ref, k_ref, v_ref, bx_ref, gx_ref, s0_ref, o_ref, sout_ref, s_ref, low_ref, qk_ref):
    c = pl.program_id(1)
    nc = pl.num_programs(1)

    @pl.when(c == 0)
    def _():
        s_ref[...] = s0_ref[...]

    ri = lax.broadcasted_iota(jnp.int32, (CHUNK, CHUNK), 0)
    ci = lax.broadcasted_iota(jnp.int32, (CHUNK, CHUNK), 1)
    causal = ri >= ci
    strict = ri > ci
    eye = (ri == ci).astype(F32)
    base = _INV_BASE
    base_blocks = (ri // base) == (ci // base)
    levels = []
    s = base
    while s < CHUNK:
        levels.append(((ri // (2 * s)) == (ci // (2 * s))) & ((ri // s) % 2 == 1) & ((ci // s) % 2 == 0))
        s *= 2

    gx = gx_ref[...]
    rw = lax.broadcasted_iota(jnp.int32, (CHUNK, LANES), 0)
    cw = lax.broadcasted_iota(jnp.int32, (CHUNK, LANES), 1)
    upper = jnp.concatenate([(rw <= cw).astype(F32)] * DN_HEADS, axis=1)
    gc_all = _dot_hi(causal.astype(F32), gx)
    gcrow_all = _dot_hi(jnp.ones((CHUNK, CHUNK), F32), gx * upper)
    heads = range(DN_HEADS)
    cs = [slice(h * LANES, (h + 1) * LANES) for h in heads]
    kk = [_dot_nt(jnp.concatenate([k_ref[:, cs[h]] * bx_ref[:, cs[h]], q_ref[:, cs[h]]], axis=0), k_ref[:, cs[h]])
          for h in heads]
    for h in heads:
        decay = jnp.exp(jnp.where(causal, gc_all[:, cs[h]][:, :CHUNK] - gcrow_all[:, cs[h]][:, :CHUNK], NEG_INF))
        low_ref[h] = jnp.where(strict, kk[h][:CHUNK] * decay, 0.0)
        qk_ref[h] = (kk[h][CHUNK:] * decay).astype(BF16)
    m = [jnp.where(base_blocks, -low_ref[h], 0.0) for h in heads]
    tmat = [eye + m[h] for h in heads]
    p = 1
    while 2 * p < base:
        m = [_dot3(m[h], m[h]) for h in heads]
        tm = [_dot3(tmat[h], m[h]) for h in heads]
        tmat = [tmat[h] + tm[h] for h in heads]
        p *= 2
    for lvl in levels:
        tc = [_dot3(tmat[h], jnp.where(lvl, low_ref[h], 0.0)) for h in heads]
        tct = [_dot3(tc[h], tmat[h]) for h in heads]
        tmat = [tmat[h] - tct[h] for h in heads]
    uw = []
    for h in heads:
        k = k_ref[:, cs[h]]
        bx = bx_ref[:, cs[h]]
        rhs = jnp.concatenate([v_ref[:, cs[h]] * bx, k * bx * jnp.exp(gc_all[:, cs[h]])], axis=1)
        uw.append(_dot(tmat[h], rhs))
    ws_qs = [_dot(jnp.concatenate([uw[h][:, DN_DV:], q_ref[:, cs[h]] * jnp.exp(gc_all[:, cs[h]])], axis=0), s_ref[h])
             for h in heads]
    v_new = [uw[h][:, :DN_DV] - ws_qs[h][:CHUNK] for h in heads]
    qv = [jnp.dot(qk_ref[h], v_new[h].astype(BF16), preferred_element_type=F32) for h in heads]
    for h in heads:
        o_ref[:, cs[h]] = ws_qs[h][CHUNK:] + qv[h]
    for h in heads:
        gc = gc_all[:, cs[h]]
        g_last = gc[CHUNK - 1:CHUNK, :]
        s_ref[h] = s_ref[h] * jnp.exp(g_last) + _dot_tn(k_ref[:, cs[h]] * jnp.exp(g_last - gc), v_new[h])

    @pl.when(c == nc - 1)
    def _():
        sout_ref[...] = s_ref[...]


def _delta(q, k, v, bx, gx, s0, nb, t):
    n = nb * t
    nc = t // CHUNK
    vw = DN_HEADS * DN_DV
    tok = pl.BlockSpec((CHUNK, vw), lambda b, c: (b * nc + c, 0))
    st = pl.BlockSpec((None, DN_HEADS, DN_DK, DN_DV), lambda b, c: (b, 0, 0, 0))
    return pl.pallas_call(
        _delta_body,
        grid=(nb, nc),
        in_specs=[tok, tok, tok, tok, tok, st],
        out_specs=[tok, st],
        out_shape=[jax.ShapeDtypeStruct((n, vw), F32),
                   jax.ShapeDtypeStruct((nb, DN_HEADS, DN_DK, DN_DV), F32)],
        scratch_shapes=[pltpu.VMEM((DN_HEADS, DN_DK, DN_DV), F32),
                        pltpu.VMEM((DN_HEADS, CHUNK, CHUNK), F32),
                        pltpu.VMEM((DN_HEADS, CHUNK, CHUNK), BF16)],
        compiler_params=_params(("parallel", "arbitrary"), 32),
        name="delta",
    )(q, k, v, bx, gx, s0)


def _merge_body(x_ref, ysc_ref, o_ref, z_ref, gsc_ref, gdn_ref, ng_ref, wsc_ref, wdn_ref, wo_ref, h_ref):
    tm = x_ref.shape[0]
    parts = []
    for h in range(DN_HEADS):
        cs = slice(h * LANES, (h + 1) * LANES)
        z = z_ref[:, cs]
        on = _rms(o_ref[:, cs], ng_ref[...]) * (z * jax.nn.sigmoid(z))
        parts.append(on.astype(BF16))
    y_dn = jnp.dot(jnp.concatenate(parts, axis=1), wdn_ref[...], preferred_element_type=F32)
    y_sc = jnp.dot(ysc_ref[...], wsc_ref[...], preferred_element_type=F32)
    merged = jax.nn.sigmoid(gsc_ref[...]) * y_sc + jax.nn.sigmoid(gdn_ref[...]) * y_dn
    h_ref[...] = x_ref[...] + jnp.dot(merged.astype(BF16), wo_ref[...], preferred_element_type=F32)


def _merge(x, ysc, o, proj, ng, wsc, wdn, wo, tm, col0_z, col0_gsc, col0_gdn):
    n, d = x.shape
    sc = ysc.shape[1]
    vw = o.shape[1]
    const = lambda shape: pl.BlockSpec(shape, lambda i: (0, 0), pipeline_mode=pl.Buffered(1))
    return pl.pallas_call(
        _merge_body,
        grid=(n // tm,),
        in_specs=[
            pl.BlockSpec((tm, d), lambda i: (i, 0)),
            pl.BlockSpec((tm, sc), lambda i: (i, 0)),
            pl.BlockSpec((tm, vw), lambda i: (i, 0)),
            pl.BlockSpec((tm, vw), lambda i: (i, col0_z // vw)),
            pl.BlockSpec((tm, d), lambda i: (i, col0_gsc // d)),
            pl.BlockSpec((tm, d), lambda i: (i, col0_gdn // d)),
            const((1, LANES)),
            const((sc, d)),
            const((vw, d)),
            const((d, d)),
        ],
        out_specs=pl.BlockSpec((tm, d), lambda i: (i, 0)),
        out_shape=jax.ShapeDtypeStruct((n, d), F32),
        compiler_params=_params(("parallel",), 48),
        name="merge",
    )(x, ysc, o, proj, proj, proj, ng, wsc, wdn, wo)


_CAND_ROWS = 10 * SUBLANES


def _tree(op, xs):
    xs = list(xs)
    while len(xs) > 1:
        xs = [op(xs[i], xs[i + 1]) for i in range(0, len(xs) - 1, 2)] + (xs[-1:] if len(xs) % 2 else [])
    return xs[0]


_NO_INDEX = float(2 ** 20)


def _topk_ranks(scores):
    nk, ln = scores[0].shape
    ng = nk // SUBLANES
    probs = range(len(scores))
    sub = lax.broadcasted_iota(jnp.int32, (SUBLANES, ln), 0).astype(F32)
    vals = [[s[g * SUBLANES:(g + 1) * SUBLANES, :] for g in range(ng)] for s in scores]
    rank = [[jnp.full((SUBLANES, ln), float(PEER_TOPK), F32) for _ in range(ng)] for _ in probs]
    base = [float(g * SUBLANES) for g in range(ng)]
    tops = [[] for _ in probs]
    for it in range(PEER_TOPK):
        m8 = [_tree(jnp.maximum, vals[p]) for p in probs]
        m = [jnp.max(m8[p], axis=0, keepdims=True) for p in probs]
        cand = [_tree(jnp.minimum, [jnp.where(vals[p][g] == m[p], base[g], _NO_INDEX) for g in range(ng)])
                for p in probs]
        first = [jnp.min(cand[p] + sub, axis=0, keepdims=True) for p in probs]
        grp = [first[p] - sub for p in probs]
        for p in probs:
            tops[p].append(m[p])
            for g in range(ng):
                sel = grp[p] == base[g]
                rank[p][g] = jnp.where(sel, float(it), rank[p][g])
                vals[p][g] = jnp.where(sel, NEG_INF, vals[p][g])
    return [(jnp.concatenate(rank[p], axis=0), jnp.concatenate(tops[p], axis=0)) for p in probs]


def _joint_select(pairs):
    ln = pairs[0][0].shape[1]
    probs = range(len(pairs))
    sub = lax.broadcasted_iota(jnp.int32, (SUBLANES, ln), 0).astype(F32)
    isub = lax.broadcasted_iota(jnp.int32, (SUBLANES, ln), 0)
    pos_of = [r * PEER_TOPK + sub for r in range(SUBLANES)]
    pos_of.append(SUBLANES + sub)
    pos_of.append((SUBLANES + sub) * PEER_TOPK)
    ngr = len(pos_of)
    vals = []
    for v1, v2 in pairs:
        blocks = []
        for r in range(SUBLANES):
            cnt = min(SUBLANES, PEER_TOPK // (r + 1))
            blocks.append(jnp.where(isub < cnt, v1[r:r + 1, :] + v2[0:SUBLANES, :], NEG_INF))
        blocks.append(v1[0:1, :] + v2[SUBLANES:, :])
        blocks.append(v1[SUBLANES:, :] + v2[0:1, :])
        vals.append(blocks)
    chosen = [[jnp.zeros((SUBLANES, ln), F32) for _ in range(ngr)] for _ in probs]
    for _ in range(PEER_TOPK):
        m8 = [_tree(jnp.maximum, vals[p]) for p in probs]
        m = [jnp.max(m8[p], axis=0, keepdims=True) for p in probs]
        pos = [[jnp.where(vals[p][g] == m[p], pos_of[g], _NO_INDEX) for g in range(ngr)] for p in probs]
        first = [jnp.min(_tree(jnp.minimum, pos[p]), axis=0, keepdims=True) for p in probs]
        for p in probs:
            for g in range(ngr):
                sel = pos[p][g] == first[p]
                chosen[p][g] = jnp.where(sel, 1.0, chosen[p][g])
                vals[p][g] = jnp.where(sel, NEG_INF, vals[p][g])
    return [jnp.concatenate(chosen[p], axis=0) for p in probs]


def _router_body(h_ref, g_ref, wqt_ref, keys_ref, xnt_ref, a_ref, b_ref, c_ref, r_ref, qt_ref, st_ref, rk_ref, vl_ref):
    tm = h_ref.shape[0]
    xn = _rms(h_ref[...], g_ref[...])
    xnt = xn.T.astype(BF16)
    xnt_ref[...] = xnt
    qt_ref[...] = jnp.dot(wqt_ref[...], xnt, preferred_element_type=F32).astype(BF16)

    for hp in range(2 * PEER_HEADS):
        q = qt_ref[hp * PEER_HALF:(hp + 1) * PEER_HALF, :]
        st_ref[hp] = jnp.dot(keys_ref[hp], q, preferred_element_type=F32)

    chunks = [slice(lc * LANES, (lc + 1) * LANES) for lc in range(tm // LANES)]

    def rank_step(hp, carry):
        for ls, (rank, vals) in zip(chunks, _topk_ranks([st_ref[hp, :, ls] for ls in chunks])):
            rk_ref[hp, :, ls] = rank
            vl_ref[hp, :, ls] = vals
        return carry

    lax.fori_loop(0, 2 * PEER_HEADS, rank_step, 0)

    def head_step(h, carry):
        tops = [(vl_ref[2 * h, :, ls], vl_ref[2 * h + 1, :, ls]) for ls in chunks]
        for ls, (v1, v2), chosen in zip(chunks, tops, _joint_select(tops)):
            e1 = jnp.exp(v1 - v1[0:1, :])
            e2 = jnp.exp(v2 - v2[0:1, :])
            z = jnp.zeros((1, LANES), F32)
            c_rows = []
            for r in range(SUBLANES):
                blk = chosen[r * SUBLANES:(r + 1) * SUBLANES, :]
                cnt = jnp.sum(blk, axis=0, keepdims=True)
                z = z + e1[r:r + 1, :] * jnp.sum(blk * e2[0:SUBLANES, :], axis=0, keepdims=True)
                if r == 0:
                    strip = chosen[SUBLANES * SUBLANES:SUBLANES * SUBLANES + SUBLANES, :]
                    cnt = cnt + jnp.sum(strip, axis=0, keepdims=True)
                    z = z + e1[0:1, :] * jnp.sum(strip * e2[SUBLANES:, :], axis=0, keepdims=True)
                c_rows.append(cnt)
            strip = chosen[SUBLANES * SUBLANES + SUBLANES:, :]
            z = z + e2[0:1, :] * jnp.sum(strip * e1[SUBLANES:, :], axis=0, keepdims=True)
            counts = jnp.concatenate(c_rows + [strip], axis=0)
            inv_z = 1.0 / z

            rank1 = rk_ref[2 * h, :, ls]
            rank2 = rk_ref[2 * h + 1, :, ls]
            in1 = rank1 < float(PEER_TOPK)
            in2 = rank2 < float(PEER_TOPK)
            a_key = jnp.exp(jnp.where(in1, st_ref[2 * h, :, ls] - v1[0:1, :], NEG_INF)) * inv_z
            b_key = jnp.exp(jnp.where(in2, st_ref[2 * h + 1, :, ls] - v2[0:1, :], NEG_INF))
            c_key = jnp.zeros((PEER_NKEYS, LANES), F32)
            for r in range(PEER_TOPK):
                c_key = jnp.where(rank1 == float(r), counts[r:r + 1, :], c_key)
            a_ref[h, :, ls] = a_key
            c_ref[h, :, ls] = c_key * GATE_STEP
            b_ref[h, :, ls] = b_key.astype(BF16)
            r_ref[h, :, ls] = (rank2 * GATE_STEP).astype(BF16)
        return carry

    lax.fori_loop(0, PEER_HEADS, head_step, 0)


def _router(h, g, wqt, keys, tm):
    n, d = h.shape
    qd = wqt.shape[0]
    n_hp = 2 * PEER_HEADS
    fac = lambda: pl.BlockSpec((PEER_HEADS, PEER_NKEYS, tm), lambda i: (0, 0, i))
    fac_shape = lambda dt: jax.ShapeDtypeStruct((PEER_HEADS, PEER_NKEYS, n), dt)
    return pl.pallas_call(
        _router_body,
        grid=(n // tm,),
        in_specs=[
            pl.BlockSpec((tm, d), lambda i: (i, 0)),
            pl.BlockSpec((1, d), lambda i: (0, 0)),
            pl.BlockSpec((qd, d), lambda i: (0, 0), pipeline_mode=pl.Buffered(1)),
            pl.BlockSpec((n_hp, PEER_NKEYS, PEER_HALF), lambda i: (0, 0, 0), pipeline_mode=pl.Buffered(1)),
        ],
        out_specs=[pl.BlockSpec((d, tm), lambda i: (0, i)), fac(), fac(), fac(), fac()],
        out_shape=[jax.ShapeDtypeStruct((d, n), BF16), fac_shape(F32), fac_shape(BF16), fac_shape(F32),
                   fac_shape(BF16)],
        scratch_shapes=[
            pltpu.VMEM((qd, tm), BF16),
            pltpu.VMEM((n_hp, PEER_NKEYS, tm), F32),
            pltpu.VMEM((n_hp, PEER_NKEYS, tm), F32),
            pltpu.VMEM((n_hp, PEER_TOPK, tm), F32),
        ],
        compiler_params=_params(("parallel",), 48),
        name="router",
    )(h, g, wqt, keys)


def _experts_body(xnt_ref, u_ref, vt_ref, a_ref, b_ref, c_ref, r_ref, h_ref, g_ref, y_ref, acc_ref,
                  hid0_ref, hid1_ref, w_ref, *, ne, final_norm):
    s = pl.program_id(0)
    n_i = u_ref.shape[0] // PEER_NKEYS
    tm = xnt_ref.shape[1]
    pk = b_ref.shape[2]
    zero = jnp.zeros((), BF16)

    @pl.when(s == 0)
    def _():
        acc_ref[...] = jnp.zeros_like(acc_ref)
        hid1_ref[...] = jnp.zeros_like(hid1_ref)

    def step(hid_next, hid_cur):
        hid_next[...] = jnp.dot(u_ref[...], xnt_ref[...], preferred_element_type=F32)
        for ii in range(n_i):
            gate = jnp.zeros((PEER_NKEYS // pk, pk, tm), BF16)
            for h in range(PEER_HEADS):
                a_i = jnp.broadcast_to(a_ref[h, ii:ii + 1, :], (pk, tm)).astype(BF16)
                c_i = jnp.broadcast_to(c_ref[h, ii:ii + 1, :], (pk, tm)).astype(BF16)
                gate = gate + jnp.minimum(a_i * b_ref[h], jnp.maximum(c_i - r_ref[h], zero))
            x = hid_cur[ii * PEER_NKEYS:(ii + 1) * PEER_NKEYS, :]
            act = 0.5 * x * (1.0 + lax.erf(x * (2.0 ** -0.5)))
            w_ref[ii * PEER_NKEYS:(ii + 1) * PEER_NKEYS, :] = act.astype(BF16) * gate.reshape(PEER_NKEYS, tm)
        acc_ref[...] += jnp.dot(vt_ref[...], w_ref[...], preferred_element_type=F32)

    @pl.when(s % 2 == 0)
    def _():
        step(hid0_ref, hid1_ref)

    @pl.when(s % 2 == 1)
    def _():
        step(hid1_ref, hid0_ref)

    @pl.when((s > 0) & (s % ne == 0))
    def _():
        y = h_ref[...] + acc_ref[...].T
        y_ref[...] = _rms(y, g_ref[...]) if final_norm else y
        acc_ref[...] = jnp.zeros_like(acc_ref)


def _experts(xnt, u, vt, a, b, c, r, h, g, tm, te, final_norm):
    d, n = xnt.shape
    n_i = te // PEER_NKEYS
    pk = b.shape[2]
    ne = u.shape[0] // te
    steps = (n // tm) * ne
    ahead = lambda s: jnp.minimum(s, steps - 1)
    cur = lambda s: jnp.maximum(s - 1, 0)
    full = lambda: pl.BlockSpec((PEER_HEADS, PEER_NKEYS // pk, pk, tm), lambda s: (0, 0, 0, cur(s) // ne))
    rows = lambda: pl.BlockSpec((PEER_HEADS, n_i, tm), lambda s: (0, cur(s) % ne, cur(s) // ne))
    return pl.pallas_call(
        functools.partial(_experts_body, ne=ne, final_norm=final_norm),
        grid=(steps + 1,),
        in_specs=[
            pl.BlockSpec((d, tm), lambda s: (0, ahead(s) // ne)),
            pl.BlockSpec((te, d), lambda s: (ahead(s) % ne, 0)),
            pl.BlockSpec((d, te), lambda s: (0, cur(s) % ne)),
            rows(), full(), rows(), full(),
            pl.BlockSpec((tm, d), lambda s: (cur(s) // ne, 0)),
            pl.BlockSpec((1, d), lambda s: (0, 0)),
        ],
        out_specs=pl.BlockSpec((tm, d), lambda s: (cur(s) // ne, 0)),
        out_shape=jax.ShapeDtypeStruct((n, d), F32),
        scratch_shapes=[pltpu.VMEM((d, tm), F32), pltpu.VMEM((te, tm), F32), pltpu.VMEM((te, tm), F32),
                        pltpu.VMEM((te, tm), BF16)],
        compiler_params=_params(("arbitrary",), 58),
        name="experts",
    )(xnt, u, vt, a, b, c, r, h, g)


def _pick(n, pref):
    t = min(pref, n)
    while n % t:
        t //= 2
    return t


def _pad_rows_front(a, rows):
    return jnp.pad(a, ((0, 0), (rows - a.shape[1], 0), (0, 0)))


def _lane_row(vals, offset):
    return jnp.zeros((1, LANES), F32).at[0, offset:offset + vals.shape[0]].set(vals.astype(F32))


def _layer(hs, states, wts, final_g):
    (norm1_g, w_in, sc_conv_w, dn_conv_w, a_log, dt_bias, dn_norm_g, w_br_sc, w_br_dn, w_o, norm2_g,
     peer_wq, peer_keys, peer_u, peer_v) = wts
    d = w_in.shape[0]
    sc = sc_conv_w.shape[1]
    qkv = dn_conv_w.shape[1]
    vw = DN_HEADS * DN_DV
    sc_taps, dn_taps = sc_conv_w.shape[0], dn_conv_w.shape[0]

    o_scb, o_scc, o_sch, o_qkv = 0, sc, 2 * sc, 3 * sc
    o_z = o_qkv + qkv
    o_beta = o_z + vw
    o_a = o_beta + DN_HEADS
    o_gsc = o_a + DN_HEADS
    o_gdn = o_gsc + d
    seg = lambda o, w: w_in[:, o:o + w]
    w_main = jnp.concatenate([seg(o_qkv, qkv), seg(o_z, vw), seg(o_gsc, d), seg(o_gdn, d),
                              seg(o_scb, sc), seg(o_scc, sc), seg(o_sch, sc)], axis=1).astype(BF16)
    col_z, col_gsc, col_gdn, col_sc = qkv, qkv + vw, qkv + vw + d, qkv + vw + 2 * d
    w_ba = jnp.pad(jnp.concatenate([seg(o_beta, DN_HEADS), seg(o_a, DN_HEADS)], axis=1),
                   ((0, 0), (0, LANES - 2 * DN_HEADS))).astype(BF16)
    scw = jnp.pad(sc_conv_w, ((0, SUBLANES - sc_taps), (0, 0)))
    dnw = jnp.pad(dn_conv_w, ((0, SUBLANES - dn_taps), (0, 0)))
    alog = _lane_row(a_log, DN_HEADS)
    dtb = _lane_row(dt_bias, DN_HEADS)
    ng = dn_norm_g.reshape(1, DN_DV).astype(F32)
    wsc, wdn, wo = w_br_sc.astype(BF16), w_br_dn.astype(BF16), w_o.astype(BF16)
    wqt = peer_wq.T.astype(BF16)
    keys = peer_keys.reshape(2 * PEER_HEADS, PEER_NKEYS, PEER_HALF).astype(BF16)
    u_tab = peer_u.astype(BF16)
    vt_tab = peer_v.T.astype(BF16)
    g1 = norm1_g.reshape(1, d)
    g2 = norm2_g.reshape(1, d)
    gf = jnp.ones((1, d), F32) if final_g is None else final_g.reshape(1, d)

    outs, new_states = [], []
    for x3, (dn_state, dn_buf, sc_buf) in zip(hs, states):
        nb, t, _ = x3.shape
        n = nb * t
        x = x3.reshape(n, d)
        proj, ba = _inproj(x, g1, w_main, w_ba, _pick(n, 1024), 1024)
        ysc, q, k, v, bx, gx, sc_new, dn_new = _prep(
            proj, ba, _pad_rows_front(sc_buf, SUBLANES), _pad_rows_front(dn_buf, SUBLANES), scw, dnw, alog, dtb,
            nb, t, _pick(t, 256), sc, qkv, col_sc, sc_taps, dn_taps)
        o, s_new = _delta(q, k, v, bx, gx, dn_state.astype(F32), nb, t)
        h1 = _merge(x, ysc, o, proj, ng, wsc, wdn, wo, _pick(n, 256), col_z, col_gsc, col_gdn)
        xnt, fa, fb, fc, fr = _router(h1, g2, wqt, keys, _pick(n, 256))
        pack = lambda f: f.reshape(PEER_HEADS, PEER_NKEYS // BF16_ROWS, BF16_ROWS, n)
        y = _experts(xnt, u_tab, vt_tab, fa, pack(fb), fc, pack(fr), h1, gf, _pick(n, 512), 1024,
                     final_g is not None)
        outs.append(y.reshape(nb, t, d))
        new_states.append((s_new.astype(dn_state.dtype), dn_new[:, SUBLANES - (dn_taps - 1):, :],
                           sc_new[:, SUBLANES - (sc_taps - 1):, :]))
    return outs, new_states


def kernel(x_prompt, x_sample, state_dn, state_dn_conv, state_sc_conv, norm1_g, w_in, sc_conv_w, dn_conv_w,
           dn_a_log, dn_dt_bias, dn_norm_g, w_br_sc, w_br_dn, w_o, norm2_g, peer_wq, peer_keys, peer_u, peer_v,
           final_norm_g):
    depth = w_in.shape[0]
    nbp = x_prompt.shape[0]
    hs = [x_prompt, x_sample]
    per_layer = []
    for l in range(depth):
        wts = (norm1_g[l], w_in[l], sc_conv_w[l], dn_conv_w[l], dn_a_log[l], dn_dt_bias[l], dn_norm_g[l],
               w_br_sc[l], w_br_dn[l], w_o[l], norm2_g[l], peer_wq[l], peer_keys[l], peer_u[l], peer_v[l])
        zero_states = (jnp.zeros((nbp, DN_HEADS, DN_DK, DN_DV), state_dn.dtype),
                       jnp.zeros((nbp, dn_conv_w.shape[1] - 1, dn_conv_w.shape[2]), x_prompt.dtype),
                       jnp.zeros((nbp, sc_conv_w.shape[1] - 1, sc_conv_w.shape[2]), x_prompt.dtype))
        states = [zero_states, (state_dn[l], state_dn_conv[l], state_sc_conv[l])]
        hs, new_states = _layer(hs, states, wts, final_norm_g if l == depth - 1 else None)
        per_layer.append(new_states)
    stack = lambda g, i: jnp.stack([per_layer[l][g][i] for l in range(depth)])
    return (hs[0], hs[1], stack(0, 0), stack(0, 1), stack(0, 2), stack(1, 0), stack(1, 1), stack(1, 2))
```

```python
import functools
import math

import jax
import jax.numpy as jnp
from jax import lax
from jax.experimental import pallas as pl
from jax.experimental.pallas import tpu as pltpu

F32 = jnp.float32
BF16 = jnp.bfloat16
EPS = 1e-6
LANES = 128
SUBLANES = 8
BF16_ROWS = 16
GATE_STEP = 2.0
CHUNK = 64
DN_HEADS = 8
DN_DK = 128
DN_DV = 128
PEER_HEADS = 8
PEER_NKEYS = 128
PEER_HALF = 128
PEER_TOPK = 16
NEG_INF = float("-inf")
HIGHEST = lax.Precision.HIGHEST
MIB = 2 ** 20


def _params(sem, vmem_mib):
    return pltpu.CompilerParams(dimension_semantics=sem, vmem_limit_bytes=vmem_mib * MIB)


def _dot(a, b):
    return jnp.dot(a.astype(BF16), b.astype(BF16), preferred_element_type=F32)


def _dot_nt(a, b):
    return lax.dot_general(a.astype(BF16), b.astype(BF16), (((1,), (1,)), ((), ())),
                           preferred_element_type=F32)


def _dot_tn(a, b):
    return lax.dot_general(a.astype(BF16), b.astype(BF16), (((0,), (0,)), ((), ())),
                           preferred_element_type=F32)


def _dot_hi(a, b):
    return jnp.dot(a, b, preferred_element_type=F32, precision=HIGHEST)


def _split(x):
    hi = x.astype(BF16)
    return hi, (x - hi.astype(F32)).astype(BF16)


def _dot3(a, b):
    ah, al = _split(a)
    bh, bl = _split(b)
    return jnp.dot(jnp.concatenate([ah, al, ah], axis=1), jnp.concatenate([bh, bh, bl], axis=0),
                   preferred_element_type=F32)


def _rms(x, g):
    return x * lax.rsqrt(jnp.mean(x * x, axis=-1, keepdims=True) + EPS) * g


def _softplus(x):
    return jnp.maximum(x, 0.0) + jnp.log1p(jnp.exp(-jnp.abs(x)))


def _inproj_body(x_ref, g_ref, w_ref, wba_ref, proj_ref, ba_ref, xn_ref):
    @pl.when(pl.program_id(1) == 0)
    def _():
        xn = _rms(x_ref[...], g_ref[...]).astype(BF16)
        xn_ref[...] = xn
        ba_ref[...] = jnp.dot(xn, wba_ref[...], preferred_element_type=F32)

    proj_ref[...] = jnp.dot(xn_ref[...], w_ref[...], preferred_element_type=F32)


def _inproj(x, g, w_main, w_ba, tm, tn):
    n, d = x.shape
    width = w_main.shape[1]
    return pl.pallas_call(
        _inproj_body,
        grid=(n // tm, width // tn),
        in_specs=[
            pl.BlockSpec((tm, d), lambda i, j: (i, 0)),
            pl.BlockSpec((1, d), lambda i, j: (0, 0)),
            pl.BlockSpec((d, tn), lambda i, j: (0, j)),
            pl.BlockSpec((d, LANES), lambda i, j: (0, 0)),
        ],
        out_specs=[
            pl.BlockSpec((tm, tn), lambda i, j: (i, j)),
            pl.BlockSpec((tm, LANES), lambda i, j: (i, 0)),
        ],
        out_shape=[jax.ShapeDtypeStruct((n, width), F32), jax.ShapeDtypeStruct((n, LANES), F32)],
        scratch_shapes=[pltpu.VMEM((tm, d), BF16)],
        compiler_params=_params(("parallel", "arbitrary"), 48),
        name="inproj",
    )(x, g, w_main, w_ba)


def _prep_body(scb_ref, scc_ref, sch_ref, qkv_ref, ba_ref, scbuf_ref, dnbuf_ref, scw_ref, dnw_ref,
               alog_ref, dtb_ref,
               ysc_ref, q_ref, k_ref, v_ref, bx_ref, gx_ref, scnew_ref, dnnew_ref,
               ext_sc, ext_dn, *, sc_taps, dn_taps):
    tt = scb_ref.shape[0]
    halo = SUBLANES

    @pl.when(pl.program_id(1) == 0)
    def _():
        ext_sc[0:halo, :] = scbuf_ref[...]
        ext_dn[0:halo, :] = dnbuf_ref[...]

    ext_sc[halo:halo + tt, :] = scc_ref[...] * sch_ref[...]
    for cg in range(scb_ref.shape[1] // LANES):
        cs = slice(cg * LANES, (cg + 1) * LANES)
        acc = None
        for i in range(sc_taps):
            r0 = halo - (sc_taps - 1) + i
            term = ext_sc[r0:r0 + tt, cs] * scw_ref[i:i + 1, cs]
            acc = term if acc is None else acc + term
        ysc_ref[:, cs] = (scb_ref[:, cs] * acc).astype(BF16)
    scnew_ref[...] = ext_sc[tt:tt + halo, :]
    ext_sc[0:halo, :] = ext_sc[tt:tt + halo, :]

    ext_dn[halo:halo + tt, :] = qkv_ref[...]
    n_qk = DN_HEADS * DN_DK // LANES
    for cg in range(qkv_ref.shape[1] // LANES):
        cs = slice(cg * LANES, (cg + 1) * LANES)
        acc = None
        for i in range(dn_taps):
            r0 = halo - (dn_taps - 1) + i
            term = ext_dn[r0:r0 + tt, cs] * dnw_ref[i:i + 1, cs]
            acc = term if acc is None else acc + term
        a = acc * jax.nn.sigmoid(acc)
        if cg < 2 * n_qk:
            a = a * lax.rsqrt(jnp.sum(a * a, axis=-1, keepdims=True) + EPS)
        if cg < n_qk:
            q_ref[:, cs] = a * (DN_DK ** -0.5)
        elif cg < 2 * n_qk:
            k_ref[:, slice((cg - n_qk) * LANES, (cg - n_qk + 1) * LANES)] = a
        else:
            v_ref[:, slice((cg - 2 * n_qk) * LANES, (cg - 2 * n_qk + 1) * LANES)] = a
    dnnew_ref[...] = ext_dn[tt:tt + halo, :]
    ext_dn[0:halo, :] = ext_dn[tt:tt + halo, :]

    ba = ba_ref[...]
    beta = jax.nn.sigmoid(ba)
    g = -jnp.exp(alog_ref[...]) * _softplus(ba + dtb_ref[...])
    for h in range(DN_HEADS):
        cs = slice(h * LANES, (h + 1) * LANES)
        bx_ref[:, cs] = jnp.broadcast_to(beta[:, h:h + 1], (tt, LANES))
        gx_ref[:, cs] = jnp.broadcast_to(g[:, DN_HEADS + h:DN_HEADS + h + 1], (tt, LANES))


def _prep(proj, ba, scbuf, dnbuf, scw, dnw, alog, dtb, nb, t, tt, sc, qkv, col0_sc, sc_taps, dn_taps):
    n = nb * t
    nt = t // tt
    vw = DN_HEADS * DN_DV
    row = lambda b, s: b * nt + s
    scblk = col0_sc // sc
    body = functools.partial(_prep_body, sc_taps=sc_taps, dn_taps=dn_taps)
    return pl.pallas_call(
        body,
        grid=(nb, nt),
        in_specs=[
            pl.BlockSpec((tt, sc), lambda b, s: (row(b, s), scblk)),
            pl.BlockSpec((tt, sc), lambda b, s: (row(b, s), scblk + 1)),
            pl.BlockSpec((tt, sc), lambda b, s: (row(b, s), scblk + 2)),
            pl.BlockSpec((tt, qkv), lambda b, s: (row(b, s), 0)),
            pl.BlockSpec((tt, LANES), lambda b, s: (row(b, s), 0)),
            pl.BlockSpec((None, SUBLANES, sc), lambda b, s: (b, 0, 0)),
            pl.BlockSpec((None, SUBLANES, qkv), lambda b, s: (b, 0, 0)),
            pl.BlockSpec((SUBLANES, sc), lambda b, s: (0, 0)),
            pl.BlockSpec((SUBLANES, qkv), lambda b, s: (0, 0)),
            pl.BlockSpec((1, LANES), lambda b, s: (0, 0)),
            pl.BlockSpec((1, LANES), lambda b, s: (0, 0)),
        ],
        out_specs=[
            pl.BlockSpec((tt, sc), lambda b, s: (row(b, s), 0)),
            pl.BlockSpec((tt, vw), lambda b, s: (row(b, s), 0)),
            pl.BlockSpec((tt, vw), lambda b, s: (row(b, s), 0)),
            pl.BlockSpec((tt, vw), lambda b, s: (row(b, s), 0)),
            pl.BlockSpec((tt, vw), lambda b, s: (row(b, s), 0)),
            pl.BlockSpec((tt, vw), lambda b, s: (row(b, s), 0)),
            pl.BlockSpec((None, SUBLANES, sc), lambda b, s: (b, 0, 0)),
            pl.BlockSpec((None, SUBLANES, qkv), lambda b, s: (b, 0, 0)),
        ],
        out_shape=[
            jax.ShapeDtypeStruct((n, sc), BF16),
            jax.ShapeDtypeStruct((n, vw), F32),
            jax.ShapeDtypeStruct((n, vw), F32),
            jax.ShapeDtypeStruct((n, vw), F32),
            jax.ShapeDtypeStruct((n, vw), F32),
            jax.ShapeDtypeStruct((n, vw), F32),
            jax.ShapeDtypeStruct((nb, SUBLANES, sc), F32),
            jax.ShapeDtypeStruct((nb, SUBLANES, qkv), F32),
        ],
        scratch_shapes=[pltpu.VMEM((tt + SUBLANES, sc), F32), pltpu.VMEM((tt + SUBLANES, qkv), F32)],
        compiler_params=_params(("parallel", "arbitrary"), 48),
        name="prep",
    )(proj, proj, proj, proj, ba, scbuf, dnbuf, scw, dnw, alog, dtb)


_INV_BASE = 8

def _delta_body(q_ref, k_ref, v_ref, bx_ref, gx_ref, s0_ref, o_ref, sout_ref, s_ref, low_ref, qk_ref):
    c = pl.program_id(1)
    nc = pl.num_programs(1)

    @pl.when(c == 0)
    def _():
        s_ref[...] = s0_ref[...].reshape(s_ref.shape)

    ri = lax.broadcasted_iota(jnp.int32, (CHUNK, CHUNK), 0)
    ci = lax.broadcasted_iota(jnp.int32, (CHUNK, CHUNK), 1)
    causal = ri >= ci
    strict = ri > ci
    eye = (ri == ci).astype(F32)
    base = _INV_BASE
    base_blocks = (ri // base) == (ci // base)
    levels = []
    s = base
    while s < CHUNK:
        levels.append(((ri // (2 * s)) == (ci // (2 * s))) & ((ri // s) % 2 == 1) & ((ci // s) % 2 == 0))
        s *= 2

    ns = q_ref.shape[0]
    rw = lax.broadcasted_iota(jnp.int32, (CHUNK, LANES), 0)
    cw = lax.broadcasted_iota(jnp.int32, (CHUNK, LANES), 1)
    upper = jnp.concatenate([(rw <= cw).astype(F32)] * DN_HEADS, axis=1)
    gc_all = [_dot_hi(causal.astype(F32), gx_ref[b]) for b in range(ns)]
    gcrow_all = [_dot_hi(jnp.ones((CHUNK, CHUNK), F32), gx_ref[b] * upper) for b in range(ns)]
    units = [(b, h) for b in range(ns) for h in range(DN_HEADS)]
    idx = range(len(units))
    cs = [slice(h * LANES, (h + 1) * LANES) for h in range(DN_HEADS)]
    q_of = lambda i: q_ref[units[i][0], :, cs[units[i][1]]]
    k_of = lambda i: k_ref[units[i][0], :, cs[units[i][1]]]
    v_of = lambda i: v_ref[units[i][0], :, cs[units[i][1]]]
    bx_of = lambda i: bx_ref[units[i][0], :, cs[units[i][1]]]
    gc_of = lambda i: gc_all[units[i][0]][:, cs[units[i][1]]]
    gcrow_of = lambda i: gcrow_all[units[i][0]][:, cs[units[i][1]]]
    kk = [_dot_nt(jnp.concatenate([k_of(i) * bx_of(i), q_of(i)], axis=0), k_of(i)) for i in idx]
    for i in idx:
        decay = jnp.exp(jnp.where(causal, gc_of(i)[:, :CHUNK] - gcrow_of(i)[:, :CHUNK], NEG_INF))
        low_ref[i] = jnp.where(strict, kk[i][:CHUNK] * decay, 0.0)
        qk_ref[i] = (kk[i][CHUNK:] * decay).astype(BF16)
    m = [jnp.where(base_blocks, -low_ref[i], 0.0) for i in idx]
    tmat = [eye + m[i] for i in idx]
    p = 1
    while 2 * p < base:
        m = [_dot3(m[i], m[i]) for i in idx]
        tm = [_dot3(tmat[i], m[i]) for i in idx]
        tmat = [tmat[i] + tm[i] for i in idx]
        p *= 2
    for lvl in levels:
        tc = [_dot3(tmat[i], jnp.where(lvl, low_ref[i], 0.0)) for i in idx]
        tct = [_dot3(tc[i], tmat[i]) for i in idx]
        tmat = [tmat[i] - tct[i] for i in idx]
    uw = []
    for i in idx:
        rhs = jnp.concatenate([v_of(i) * bx_of(i), k_of(i) * bx_of(i) * jnp.exp(gc_of(i))], axis=1)
        uw.append(_dot(tmat[i], rhs))
    ws_qs = [_dot(jnp.concatenate([uw[i][:, DN_DV:], q_of(i) * jnp.exp(gc_of(i))], axis=0), s_ref[i])
             for i in idx]
    v_new = [uw[i][:, :DN_DV] - ws_qs[i][:CHUNK] for i in idx]
    qv = [jnp.dot(qk_ref[i], v_new[i].astype(BF16), preferred_element_type=F32) for i in idx]
    for i in idx:
        o_ref[units[i][0], :, cs[units[i][1]]] = ws_qs[i][CHUNK:] + qv[i]
    for i in idx:
        gc = gc_of(i)
        g_last = gc[CHUNK - 1:CHUNK, :]
        s_ref[i] = s_ref[i] * jnp.exp(g_last) + _dot_tn(k_of(i) * jnp.exp(g_last - gc), v_new[i])

    @pl.when(c == nc - 1)
    def _():
        sout_ref[...] = s_ref[...].reshape(sout_ref.shape)


def _delta(q, k, v, bx, gx, s0, nb, t):
    nc = t // CHUNK
    vw = DN_HEADS * DN_DV
    ns = 2 if nb % 2 == 0 else 1
    tok = pl.BlockSpec((ns, CHUNK, vw), lambda b, c: (b, c, 0))
    st = pl.BlockSpec((ns, DN_HEADS, DN_DK, DN_DV), lambda b, c: (b, 0, 0, 0))
    seq = lambda a: a.reshape(nb, t, vw)
    o, s_new = pl.pallas_call(
        _delta_body,
        grid=(nb // ns, nc),
        in_specs=[tok, tok, tok, tok, tok, st],
        out_specs=[tok, st],
        out_shape=[jax.ShapeDtypeStruct((nb, t, vw), F32),
                   jax.ShapeDtypeStruct((nb, DN_HEADS, DN_DK, DN_DV), F32)],
        scratch_shapes=[pltpu.VMEM((ns * DN_HEADS, DN_DK, DN_DV), F32),
                        pltpu.VMEM((ns * DN_HEADS, CHUNK, CHUNK), F32),
                        pltpu.VMEM((ns * DN_HEADS, CHUNK, CHUNK), BF16)],
        compiler_params=_params(("parallel", "arbitrary"), 32),
        name="delta",
    )(seq(q), seq(k), seq(v), seq(bx), seq(gx), s0)
    return o.reshape(nb * t, vw), s_new


def _merge_body(x_ref, ysc_ref, o_ref, z_ref, gsc_ref, gdn_ref, ng_ref, wsc_ref, wdn_ref, wo_ref, h_ref):
    tm = x_ref.shape[0]
    parts = []
    for h in range(DN_HEADS):
        cs = slice(h * LANES, (h + 1) * LANES)
        z = z_ref[:, cs]
        on = _rms(o_ref[:, cs], ng_ref[...]) * (z * jax.nn.sigmoid(z))
        parts.append(on.astype(BF16))
    y_dn = jnp.dot(jnp.concatenate(parts, axis=1), wdn_ref[...], preferred_element_type=F32)
    y_sc = jnp.dot(ysc_ref[...], wsc_ref[...], preferred_element_type=F32)
    merged = jax.nn.sigmoid(gsc_ref[...]) * y_sc + jax.nn.sigmoid(gdn_ref[...]) * y_dn
    h_ref[...] = x_ref[...] + jnp.dot(merged.astype(BF16), wo_ref[...], preferred_element_type=F32)


def _merge(x, ysc, o, proj, ng, wsc, wdn, wo, tm, col0_z, col0_gsc, col0_gdn):
    n, d = x.shape
    sc = ysc.shape[1]
    vw = o.shape[1]
    const = lambda shape: pl.BlockSpec(shape, lambda i: (0, 0), pipeline_mode=pl.Buffered(1))
    return pl.pallas_call(
        _merge_body,
        grid=(n // tm,),
        in_specs=[
            pl.BlockSpec((tm, d), lambda i: (i, 0)),
            pl.BlockSpec((tm, sc), lambda i: (i, 0)),
            pl.BlockSpec((tm, vw), lambda i: (i, 0)),
            pl.BlockSpec((tm, vw), lambda i: (i, col0_z // vw)),
            pl.BlockSpec((tm, d), lambda i: (i, col0_gsc // d)),
            pl.BlockSpec((tm, d), lambda i: (i, col0_gdn // d)),
            const((1, LANES)),
            const((sc, d)),
            const((vw, d)),
            const((d, d)),
        ],
        out_specs=pl.BlockSpec((tm, d), lambda i: (i, 0)),
        out_shape=jax.ShapeDtypeStruct((n, d), F32),
        compiler_params=_params(("parallel",), 48),
        name="merge",
    )(x, ysc, o, proj, proj, proj, ng, wsc, wdn, wo)


_CAND_ROWS = 10 * SUBLANES


def _tree(op, xs):
    xs = list(xs)
    while len(xs) > 1:
        xs = [op(xs[i], xs[i + 1]) for i in range(0, len(xs) - 1, 2)] + (xs[-1:] if len(xs) % 2 else [])
    return xs[0]


_NO_INDEX = float(2 ** 20)


def _any_lane(flags):
    return jnp.max(_tree(jnp.maximum, flags)) > 0.5


def _topk_ranks(scores, exact):
    nk, ln = scores[0].shape
    ng = nk // SUBLANES
    probs = range(len(scores))
    sub = lax.broadcasted_iota(jnp.int32, (SUBLANES, ln), 0).astype(F32)
    vals = [[s[g * SUBLANES:(g + 1) * SUBLANES, :] for g in range(ng)] for s in scores]
    rank = [[jnp.full((SUBLANES, ln), float(PEER_TOPK), F32) for _ in range(ng)] for _ in probs]
    base = [float(g * SUBLANES) for g in range(ng)]
    tops = [[] for _ in probs]
    for it in range(PEER_TOPK):
        m8 = [_tree(jnp.maximum, vals[p]) for p in probs]
        m = [jnp.max(m8[p], axis=0, keepdims=True) for p in probs]
        if exact:
            cand = [_tree(jnp.minimum, [jnp.where(vals[p][g] == m[p], base[g], _NO_INDEX) for g in range(ng)])
                    for p in probs]
            first = [jnp.min(cand[p] + sub, axis=0, keepdims=True) for p in probs]
            grp = [first[p] - sub for p in probs]
        for p in probs:
            tops[p].append(m[p])
            for g in range(ng):
                sel = (grp[p] == base[g]) if exact else (vals[p][g] == m[p])
                rank[p][g] = jnp.where(sel, float(it), rank[p][g])
                vals[p][g] = jnp.where(sel, NEG_INF, vals[p][g])
    out = []
    for p in probs:
        if exact:
            bad = jnp.zeros((1, ln), F32)
        else:
            ranked = _tree(jnp.add, [jnp.where(rank[p][g] < float(PEER_TOPK), 1.0, 0.0) for g in range(ng)])
            bad = jnp.where(jnp.sum(ranked, axis=0, keepdims=True) != float(PEER_TOPK), 1.0, 0.0)
        out.append((jnp.concatenate(rank[p], axis=0), jnp.concatenate(tops[p], axis=0), bad))
    return out


def _joint_select(pairs, exact):
    ln = pairs[0][0].shape[1]
    probs = range(len(pairs))
    sub = lax.broadcasted_iota(jnp.int32, (SUBLANES, ln), 0).astype(F32)
    isub = lax.broadcasted_iota(jnp.int32, (SUBLANES, ln), 0)
    pos_of = [r * PEER_TOPK + sub for r in range(SUBLANES)]
    pos_of.append(SUBLANES + sub)
    pos_of.append((SUBLANES + sub) * PEER_TOPK)
    ngr = len(pos_of)
    vals = []
    for v1, v2 in pairs:
        blocks = []
        for r in range(SUBLANES):
            cnt = min(SUBLANES, PEER_TOPK // (r + 1))
            blocks.append(jnp.where(isub < cnt, v1[r:r + 1, :] + v2[0:SUBLANES, :], NEG_INF))
        blocks.append(v1[0:1, :] + v2[SUBLANES:, :])
        blocks.append(v1[SUBLANES:, :] + v2[0:1, :])
        vals.append(blocks)
    chosen = [[jnp.zeros((SUBLANES, ln), F32) for _ in range(ngr)] for _ in probs]
    for _ in range(PEER_TOPK):
        m8 = [_tree(jnp.maximum, vals[p]) for p in probs]
        m = [jnp.max(m8[p], axis=0, keepdims=True) for p in probs]
        if exact:
            pos = [[jnp.where(vals[p][g] == m[p], pos_of[g], _NO_INDEX) for g in range(ngr)] for p in probs]
            first = [jnp.min(_tree(jnp.minimum, pos[p]), axis=0, keepdims=True) for p in probs]
        for p in probs:
            for g in range(ngr):
                sel = (pos[p][g] == first[p]) if exact else (vals[p][g] == m[p])
                chosen[p][g] = jnp.where(sel, 1.0, chosen[p][g])
                vals[p][g] = jnp.where(sel, NEG_INF, vals[p][g])
    out = []
    for p in probs:
        if exact:
            bad = jnp.zeros((1, ln), F32)
        else:
            n_chosen = jnp.sum(_tree(jnp.add, chosen[p]), axis=0, keepdims=True)
            bad = jnp.where(n_chosen != float(PEER_TOPK), 1.0, 0.0)
        out.append((jnp.concatenate(chosen[p], axis=0), bad))
    return out


def _router_body(h_ref, g_ref, wqt_ref, keys_ref, xnt_ref, a_ref, b_ref, c_ref, r_ref, qt_ref, st_ref, rk_ref, vl_ref,
                 ch_ref):
    tm = h_ref.shape[0]
    xn = _rms(h_ref[...], g_ref[...])
    xnt = xn.T.astype(BF16)
    xnt_ref[...] = xnt
    qt_ref[...] = jnp.dot(wqt_ref[...], xnt, preferred_element_type=F32).astype(BF16)

    for hp in range(2 * PEER_HEADS):
        q = qt_ref[hp * PEER_HALF:(hp + 1) * PEER_HALF, :]
        st_ref[hp] = jnp.dot(keys_ref[hp], q, preferred_element_type=F32)

    chunks = [slice(lc * LANES, (lc + 1) * LANES) for lc in range(tm // LANES)]

    def rank_step(hp, carry):
        def run(exact):
            res = _topk_ranks([st_ref[hp, :, ls] for ls in chunks], exact)
            for ls, (rank, vals, _) in zip(chunks, res):
                rk_ref[hp, :, ls] = rank
                vl_ref[hp, :, ls] = vals
            return _any_lane([bad for _, _, bad in res])

        @pl.when(run(False))
        def _():
            run(True)

        return carry

    lax.fori_loop(0, 2 * PEER_HEADS, rank_step, 0)

    def head_step(h, carry):
        tops = [(vl_ref[2 * h, :, ls], vl_ref[2 * h + 1, :, ls]) for ls in chunks]

        def run(exact):
            res = _joint_select(tops, exact)
            for ls, (chosen, _) in zip(chunks, res):
                ch_ref[:, ls] = chosen
            return _any_lane([bad for _, bad in res])

        @pl.when(run(False))
        def _():
            run(True)

        for ls, (v1, v2) in zip(chunks, tops):
            chosen = ch_ref[:, ls]
            e1 = jnp.exp(v1 - v1[0:1, :])
            e2 = jnp.exp(v2 - v2[0:1, :])
            z = jnp.zeros((1, LANES), F32)
            c_rows = []
            for r in range(SUBLANES):
                blk = chosen[r * SUBLANES:(r + 1) * SUBLANES, :]
                cnt = jnp.sum(blk, axis=0, keepdims=True)
                z = z + e1[r:r + 1, :] * jnp.sum(blk * e2[0:SUBLANES, :], axis=0, keepdims=True)
                if r == 0:
                    strip = chosen[SUBLANES * SUBLANES:SUBLANES * SUBLANES + SUBLANES, :]
                    cnt = cnt + jnp.sum(strip, axis=0, keepdims=True)
                    z = z + e1[0:1, :] * jnp.sum(strip * e2[SUBLANES:, :], axis=0, keepdims=True)
                c_rows.append(cnt)
            strip = chosen[SUBLANES * SUBLANES + SUBLANES:, :]
            z = z + e2[0:1, :] * jnp.sum(strip * e1[SUBLANES:, :], axis=0, keepdims=True)
            counts = jnp.concatenate(c_rows + [strip], axis=0)
            inv_z = 1.0 / z

            rank1 = rk_ref[2 * h, :, ls]
            rank2 = rk_ref[2 * h + 1, :, ls]
            in1 = rank1 < float(PEER_TOPK)
            in2 = rank2 < float(PEER_TOPK)
            a_key = jnp.exp(jnp.where(in1, st_ref[2 * h, :, ls] - v1[0:1, :], NEG_INF)) * inv_z
            b_key = jnp.exp(jnp.where(in2, st_ref[2 * h + 1, :, ls] - v2[0:1, :], NEG_INF))
            c_key = jnp.zeros((PEER_NKEYS, LANES), F32)
            for r in range(PEER_TOPK):
                c_key = jnp.where(rank1 == float(r), counts[r:r + 1, :], c_key)
            a_ref[h, :, ls] = a_key
            c_ref[h, :, ls] = c_key * GATE_STEP
            b_ref[h, :, ls] = b_key.astype(BF16)
            r_ref[h, :, ls] = (rank2 * GATE_STEP).astype(BF16)
        return carry

    lax.fori_loop(0, PEER_HEADS, head_step, 0)


def _router(h, g, wqt, keys, tm):
    n, d = h.shape
    qd = wqt.shape[0]
    n_hp = 2 * PEER_HEADS
    fac = lambda: pl.BlockSpec((PEER_HEADS, PEER_NKEYS, tm), lambda i: (0, 0, i))
    fac_shape = lambda dt: jax.ShapeDtypeStruct((PEER_HEADS, PEER_NKEYS, n), dt)
    return pl.pallas_call(
        _router_body,
        grid=(n // tm,),
        in_specs=[
            pl.BlockSpec((tm, d), lambda i: (i, 0)),
            pl.BlockSpec((1, d), lambda i: (0, 0)),
            pl.BlockSpec((qd, d), lambda i: (0, 0), pipeline_mode=pl.Buffered(1)),
            pl.BlockSpec((n_hp, PEER_NKEYS, PEER_HALF), lambda i: (0, 0, 0), pipeline_mode=pl.Buffered(1)),
        ],
        out_specs=[pl.BlockSpec((d, tm), lambda i: (0, i)), fac(), fac(), fac(), fac()],
        out_shape=[jax.ShapeDtypeStruct((d, n), BF16), fac_shape(F32), fac_shape(BF16), fac_shape(F32),
                   fac_shape(BF16)],
        scratch_shapes=[
            pltpu.VMEM((qd, tm), BF16),
            pltpu.VMEM((n_hp, PEER_NKEYS, tm), F32),
            pltpu.VMEM((n_hp, PEER_NKEYS, tm), F32),
            pltpu.VMEM((n_hp, PEER_TOPK, tm), F32),
            pltpu.VMEM((_CAND_ROWS, tm), F32),
        ],
        compiler_params=_params(("parallel",), 48),
        name="router",
    )(h, g, wqt, keys)


def _experts_body(xnt_ref, u_ref, vt_ref, a_ref, b_ref, c_ref, r_ref, h_ref, g_ref, y_ref, acc_ref,
                  hid0_ref, hid1_ref, w_ref, *, ne, final_norm):
    s = pl.program_id(0)
    n_i = u_ref.shape[0] // PEER_NKEYS
    tm = xnt_ref.shape[1]
    pk = b_ref.shape[2]
    zero = jnp.zeros((), BF16)

    @pl.when(s == 0)
    def _():
        acc_ref[...] = jnp.zeros_like(acc_ref)
        hid1_ref[...] = jnp.zeros_like(hid1_ref)

    def step(hid_next, hid_cur):
        hid_next[...] = jnp.dot(u_ref[...], xnt_ref[...], preferred_element_type=F32)
        for ii in range(n_i):
            gate = jnp.zeros((PEER_NKEYS // pk, pk, tm), BF16)
            for h in range(PEER_HEADS):
                a_i = jnp.broadcast_to(a_ref[h, ii:ii + 1, :], (pk, tm)).astype(BF16)
                c_i = jnp.broadcast_to(c_ref[h, ii:ii + 1, :], (pk, tm)).astype(BF16)
                gate = gate + jnp.minimum(a_i * b_ref[h], jnp.maximum(c_i - r_ref[h], zero))
            x = hid_cur[ii * PEER_NKEYS:(ii + 1) * PEER_NKEYS, :]
            act = 0.5 * x * (1.0 + lax.erf(x * (2.0 ** -0.5)))
            w_ref[ii * PEER_NKEYS:(ii + 1) * PEER_NKEYS, :] = act.astype(BF16) * gate.reshape(PEER_NKEYS, tm)
        acc_ref[...] += jnp.dot(vt_ref[...], w_ref[...], preferred_element_type=F32)

    @pl.when(s % 2 == 0)
    def _():
        step(hid0_ref, hid1_ref)

    @pl.when(s % 2 == 1)
    def _():
        step(hid1_ref, hid0_ref)

    @pl.when((s > 0) & (s % ne == 0))
    def _():
        y = h_ref[...] + acc_ref[...].T
        y_ref[...] = _rms(y, g_ref[...]) if final_norm else y
        acc_ref[...] = jnp.zeros_like(acc_ref)


def _experts(xnt, u, vt, a, b, c, r, h, g, tm, te, final_norm):
    d, n = xnt.shape
    n_i = te // PEER_NKEYS
    pk = b.shape[2]
    ne = u.shape[0] // te
    steps = (n // tm) * ne
    ahead = lambda s: jnp.minimum(s, steps - 1)
    cur = lambda s: jnp.maximum(s - 1, 0)
    full = lambda: pl.BlockSpec((PEER_HEADS, PEER_NKEYS // pk, pk, tm), lambda s: (0, 0, 0, cur(s) // ne))
    rows = lambda: pl.BlockSpec((PEER_HEADS, n_i, tm), lambda s: (0, cur(s) % ne, cur(s) // ne))
    return pl.pallas_call(
        functools.partial(_experts_body, ne=ne, final_norm=final_norm),
        grid=(steps + 1,),
        in_specs=[
            pl.BlockSpec((d, tm), lambda s: (0, ahead(s) // ne)),
            pl.BlockSpec((te, d), lambda s: (ahead(s) % ne, 0)),
            pl.BlockSpec((d, te), lambda s: (0, cur(s) % ne)),
            rows(), full(), rows(), full(),
            pl.BlockSpec((tm, d), lambda s: (cur(s) // ne, 0)),
            pl.BlockSpec((1, d), lambda s: (0, 0)),
        ],
        out_specs=pl.BlockSpec((tm, d), lambda s: (cur(s) // ne, 0)),
        out_shape=jax.ShapeDtypeStruct((n, d), F32),
        scratch_shapes=[pltpu.VMEM((d, tm), F32), pltpu.VMEM((te, tm), F32), pltpu.VMEM((te, tm), F32),
                        pltpu.VMEM((te, tm), BF16)],
        compiler_params=_params(("arbitrary",), 58),
        name="experts",
    )(xnt, u, vt, a, b, c, r, h, g)


def _pick(n, pref):
    t = min(pref, n)
    while n % t:
        t //= 2
    return t


def _pad_rows_front(a, rows):
    return jnp.pad(a, ((0, 0), (rows - a.shape[1], 0), (0, 0)))


def _lane_row(vals, offset):
    return jnp.zeros((1, LANES), F32).at[0, offset:offset + vals.shape[0]].set(vals.astype(F32))


def _layer(hs, states, wts, final_g):
    (norm1_g, w_in, sc_conv_w, dn_conv_w, a_log, dt_bias, dn_norm_g, w_br_sc, w_br_dn, w_o, norm2_g,
     peer_wq, peer_keys, peer_u, peer_v) = wts
    d = w_in.shape[0]
    sc = sc_conv_w.shape[1]
    qkv = dn_conv_w.shape[1]
    vw = DN_HEADS * DN_DV
    sc_taps, dn_taps = sc_conv_w.shape[0], dn_conv_w.shape[0]

    o_scb, o_scc, o_sch, o_qkv = 0, sc, 2 * sc, 3 * sc
    o_z = o_qkv + qkv
    o_beta = o_z + vw
    o_a = o_beta + DN_HEADS
    o_gsc = o_a + DN_HEADS
    o_gdn = o_gsc + d
    seg = lambda o, w: w_in[:, o:o + w]
    w_main = jnp.concatenate([seg(o_qkv, qkv), seg(o_z, vw), seg(o_gsc, d), seg(o_gdn, d),
                              seg(o_scb, sc), seg(o_scc, sc), seg(o_sch, sc)], axis=1).astype(BF16)
    col_z, col_gsc, col_gdn, col_sc = qkv, qkv + vw, qkv + vw + d, qkv + vw + 2 * d
    w_ba = jnp.pad(jnp.concatenate([seg(o_beta, DN_HEADS), seg(o_a, DN_HEADS)], axis=1),
                   ((0, 0), (0, LANES - 2 * DN_HEADS))).astype(BF16)
    scw = jnp.pad(sc_conv_w, ((0, SUBLANES - sc_taps), (0, 0)))
    dnw = jnp.pad(dn_conv_w, ((0, SUBLANES - dn_taps), (0, 0)))
    alog = _lane_row(a_log, DN_HEADS)
    dtb = _lane_row(dt_bias, DN_HEADS)
    ng = dn_norm_g.reshape(1, DN_DV).astype(F32)
    wsc, wdn, wo = w_br_sc.astype(BF16), w_br_dn.astype(BF16), w_o.astype(BF16)
    wqt = peer_wq.T.astype(BF16)
    keys = peer_keys.reshape(2 * PEER_HEADS, PEER_NKEYS, PEER_HALF).astype(BF16)
    u_tab = peer_u.astype(BF16)
    vt_tab = peer_v.T.astype(BF16)
    g1 = norm1_g.reshape(1, d)
    g2 = norm2_g.reshape(1, d)
    gf = jnp.ones((1, d), F32) if final_g is None else final_g.reshape(1, d)

    outs, new_states = [], []
    for x3, (dn_state, dn_buf, sc_buf) in zip(hs, states):
        nb, t, _ = x3.shape
        n = nb * t
        x = x3.reshape(n, d)
        proj, ba = _inproj(x, g1, w_main, w_ba, _pick(n, 1024), 1024)
        ysc, q, k, v, bx, gx, sc_new, dn_new = _prep(
            proj, ba, _pad_rows_front(sc_buf, SUBLANES), _pad_rows_front(dn_buf, SUBLANES), scw, dnw, alog, dtb,
            nb, t, _pick(t, 256), sc, qkv, col_sc, sc_taps, dn_taps)
        o, s_new = _delta(q, k, v, bx, gx, dn_state.astype(F32), nb, t)
        h1 = _merge(x, ysc, o, proj, ng, wsc, wdn, wo, _pick(n, 256), col_z, col_gsc, col_gdn)
        xnt, fa, fb, fc, fr = _router(h1, g2, wqt, keys, _pick(n, 256))
        pack = lambda f: f.reshape(PEER_HEADS, PEER_NKEYS // BF16_ROWS, BF16_ROWS, n)
        y = _experts(xnt, u_tab, vt_tab, fa, pack(fb), fc, pack(fr), h1, gf, _pick(n, 512), 1024,
                     final_g is not None)
        outs.append(y.reshape(nb, t, d))
        new_states.append((s_new.astype(dn_state.dtype), dn_new[:, SUBLANES - (dn_taps - 1):, :],
                           sc_new[:, SUBLANES - (sc_taps - 1):, :]))
    return outs, new_states


def kernel(x_prompt, x_sample, state_dn, state_dn_conv, state_sc_conv, norm1_g, w_in, sc_conv_w, dn_conv_w,
           dn_a_log, dn_dt_bias, dn_norm_g, w_br_sc, w_br_dn, w_o, norm2_g, peer_wq, peer_keys, peer_u, peer_v,
           final_norm_g):
    depth = w_in.shape[0]
    nbp = x_prompt.shape[0]
    hs = [x_prompt, x_sample]
    per_layer = []
    for l in range(depth):
        wts = (norm1_g[l], w_in[l], sc_conv_w[l], dn_conv_w[l], dn_a_log[l], dn_dt_bias[l], dn_norm_g[l],
               w_br_sc[l], w_br_dn[l], w_o[l], norm2_g[l], peer_wq[l], peer_keys[l], peer_u[l], peer_v[l])
        zero_states = (jnp.zeros((nbp, DN_HEADS, DN_DK, DN_DV), state_dn.dtype),
                       jnp.zeros((nbp, dn_conv_w.shape[1] - 1, dn_conv_w.shape[2]), x_prompt.dtype),
                       jnp.zeros((nbp, sc_conv_w.shape[1] - 1, sc_conv_w.shape[2]), x_prompt.dtype))
        states = [zero_states, (state_dn[l], state_dn_conv[l], state_sc_conv[l])]
        hs, new_states = _layer(hs, states, wts, final_norm_g if l == depth - 1 else None)
        per_layer.append(new_states)
    stack = lambda g, i: jnp.stack([per_layer[l][g][i] for l in range(depth)])
    return (hs[0], hs[1], stack(0, 0), stack(0, 1), stack(0, 2), stack(1, 0), stack(1, 1), stack(1, 2))
```

```python
import functools
import math

import jax
import jax.numpy as jnp
from jax import lax
from jax.experimental import pallas as pl
from jax.experimental.pallas import tpu as pltpu

F32 = jnp.float32
BF16 = jnp.bfloat16
EPS = 1e-6
LANES = 128
SUBLANES = 8
BF16_ROWS = 16
GATE_STEP = 2.0
PROJ_TILE = 1024
EXPERT_TILE = 1024
CHUNK = 64
DN_HEADS = 8
DN_DK = 128
DN_DV = 128
PEER_HEADS = 8
PEER_NKEYS = 128
PEER_HALF = 128
PEER_TOPK = 16
NEG_INF = float("-inf")
HIGHEST = lax.Precision.HIGHEST
MIB = 2 ** 20


def _params(sem, vmem_mib):
    return pltpu.CompilerParams(dimension_semantics=sem, vmem_limit_bytes=vmem_mib * MIB)


def _dot(a, b):
    return jnp.dot(a.astype(BF16), b.astype(BF16), preferred_element_type=F32)


def _dot_nt(a, b):
    return lax.dot_general(a.astype(BF16), b.astype(BF16), (((1,), (1,)), ((), ())),
                           preferred_element_type=F32)


def _dot_tn(a, b):
    return lax.dot_general(a.astype(BF16), b.astype(BF16), (((0,), (0,)), ((), ())),
                           preferred_element_type=F32)


def _dot_hi(a, b):
    return jnp.dot(a, b, preferred_element_type=F32, precision=HIGHEST)


def _split(x):
    hi = x.astype(BF16)
    return hi, (x - hi.astype(F32)).astype(BF16)


def _dot3(a, b):
    ah, al = _split(a)
    bh, bl = _split(b)
    return jnp.dot(jnp.concatenate([ah, al, ah], axis=1), jnp.concatenate([bh, bh, bl], axis=0),
                   preferred_element_type=F32)


def _rms(x, g):
    return x * lax.rsqrt(jnp.mean(x * x, axis=-1, keepdims=True) + EPS) * g


def _softplus(x):
    return jnp.maximum(x, 0.0) + jnp.log1p(jnp.exp(-jnp.abs(x)))


def _inproj_body(x_ref, g_ref, w_ref, wba_ref, proj_ref, ba_ref, xn_ref):
    @pl.when(pl.program_id(1) == 0)
    def _():
        xn = _rms(x_ref[...], g_ref[...]).astype(BF16)
        xn_ref[...] = xn
        ba_ref[...] = jnp.dot(xn, wba_ref[...], preferred_element_type=F32)

    proj_ref[...] = jnp.dot(xn_ref[...], w_ref[...], preferred_element_type=F32)


def _inproj(x, g, w_main, w_ba, tm):
    n, d = x.shape
    n_tiles, _, tn = w_main.shape
    width = n_tiles * tn
    return pl.pallas_call(
        _inproj_body,
        grid=(n // tm, n_tiles),
        in_specs=[
            pl.BlockSpec((tm, d), lambda i, j: (i, 0)),
            pl.BlockSpec((1, d), lambda i, j: (0, 0)),
            pl.BlockSpec((None, d, tn), lambda i, j: (j, 0, 0)),
            pl.BlockSpec((d, LANES), lambda i, j: (0, 0)),
        ],
        out_specs=[
            pl.BlockSpec((tm, tn), lambda i, j: (i, j)),
            pl.BlockSpec((tm, LANES), lambda i, j: (i, 0)),
        ],
        out_shape=[jax.ShapeDtypeStruct((n, width), F32), jax.ShapeDtypeStruct((n, LANES), F32)],
        scratch_shapes=[pltpu.VMEM((tm, d), BF16)],
        compiler_params=_params(("parallel", "arbitrary"), 48),
        name="inproj",
    )(x, g, w_main, w_ba)


def _prep_body(scb_ref, scc_ref, sch_ref, qkv_ref, ba_ref, scbuf_ref, dnbuf_ref, scw_ref, dnw_ref,
               alog_ref, dtb_ref,
               ysc_ref, q_ref, k_ref, v_ref, bg_ref, scnew_ref, dnnew_ref,
               ext_sc, ext_dn, *, sc_taps, dn_taps):
    tt = scb_ref.shape[0]
    halo = SUBLANES

    @pl.when(pl.program_id(1) == 0)
    def _():
        ext_sc[0:halo, :] = scbuf_ref[...]
        ext_dn[0:halo, :] = dnbuf_ref[...]

    ext_sc[halo:halo + tt, :] = scc_ref[...] * sch_ref[...]
    for cg in range(scb_ref.shape[1] // LANES):
        cs = slice(cg * LANES, (cg + 1) * LANES)
        acc = None
        for i in range(sc_taps):
            r0 = halo - (sc_taps - 1) + i
            term = ext_sc[r0:r0 + tt, cs] * scw_ref[i:i + 1, cs]
            acc = term if acc is None else acc + term
        ysc_ref[:, cs] = (scb_ref[:, cs] * acc).astype(BF16)
    scnew_ref[...] = ext_sc[tt:tt + halo, :]
    ext_sc[0:halo, :] = ext_sc[tt:tt + halo, :]

    ext_dn[halo:halo + tt, :] = qkv_ref[...]
    n_qk = DN_HEADS * DN_DK // LANES
    for cg in range(qkv_ref.shape[1] // LANES):
        cs = slice(cg * LANES, (cg + 1) * LANES)
        acc = None
        for i in range(dn_taps):
            r0 = halo - (dn_taps - 1) + i
            term = ext_dn[r0:r0 + tt, cs] * dnw_ref[i:i + 1, cs]
            acc = term if acc is None else acc + term
        a = acc * jax.nn.sigmoid(acc)
        if cg < 2 * n_qk:
            a = a * lax.rsqrt(jnp.sum(a * a, axis=-1, keepdims=True) + EPS)
        if cg < n_qk:
            q_ref[:, cs] = a * (DN_DK ** -0.5)
        elif cg < 2 * n_qk:
            k_ref[:, slice((cg - n_qk) * LANES, (cg - n_qk + 1) * LANES)] = a
        else:
            v_ref[:, slice((cg - 2 * n_qk) * LANES, (cg - 2 * n_qk + 1) * LANES)] = a
    dnnew_ref[...] = ext_dn[tt:tt + halo, :]
    ext_dn[0:halo, :] = ext_dn[tt:tt + halo, :]

    ba = ba_ref[...]
    lane = lax.broadcasted_iota(jnp.int32, ba.shape, 1)
    bg_ref[...] = jnp.where(lane < DN_HEADS, jax.nn.sigmoid(ba),
                            -jnp.exp(alog_ref[...]) * _softplus(ba + dtb_ref[...]))


def _prep(proj, ba, scbuf, dnbuf, scw, dnw, alog, dtb, nb, t, tt, sc, qkv, col0_sc, sc_taps, dn_taps):
    n = nb * t
    nt = t // tt
    vw = DN_HEADS * DN_DV
    row = lambda b, s: b * nt + s
    scblk = col0_sc // sc
    body = functools.partial(_prep_body, sc_taps=sc_taps, dn_taps=dn_taps)
    return pl.pallas_call(
        body,
        grid=(nb, nt),
        in_specs=[
            pl.BlockSpec((tt, sc), lambda b, s: (row(b, s), scblk)),
            pl.BlockSpec((tt, sc), lambda b, s: (row(b, s), scblk + 1)),
            pl.BlockSpec((tt, sc), lambda b, s: (row(b, s), scblk + 2)),
            pl.BlockSpec((tt, qkv), lambda b, s: (row(b, s), 0)),
            pl.BlockSpec((tt, LANES), lambda b, s: (row(b, s), 0)),
            pl.BlockSpec((None, SUBLANES, sc), lambda b, s: (b, 0, 0)),
            pl.BlockSpec((None, SUBLANES, qkv), lambda b, s: (b, 0, 0)),
            pl.BlockSpec((SUBLANES, sc), lambda b, s: (0, 0)),
            pl.BlockSpec((SUBLANES, qkv), lambda b, s: (0, 0)),
            pl.BlockSpec((1, LANES), lambda b, s: (0, 0)),
            pl.BlockSpec((1, LANES), lambda b, s: (0, 0)),
        ],
        out_specs=[
            pl.BlockSpec((tt, sc), lambda b, s: (row(b, s), 0)),
            pl.BlockSpec((tt, vw), lambda b, s: (row(b, s), 0)),
            pl.BlockSpec((tt, vw), lambda b, s: (row(b, s), 0)),
            pl.BlockSpec((tt, vw), lambda b, s: (row(b, s), 0)),
            pl.BlockSpec((tt, LANES), lambda b, s: (row(b, s), 0)),
            pl.BlockSpec((None, SUBLANES, sc), lambda b, s: (b, 0, 0)),
            pl.BlockSpec((None, SUBLANES, qkv), lambda b, s: (b, 0, 0)),
        ],
        out_shape=[
            jax.ShapeDtypeStruct((n, sc), BF16),
            jax.ShapeDtypeStruct((n, vw), F32),
            jax.ShapeDtypeStruct((n, vw), F32),
            jax.ShapeDtypeStruct((n, vw), F32),
            jax.ShapeDtypeStruct((n, LANES), F32),
            jax.ShapeDtypeStruct((nb, SUBLANES, sc), F32),
            jax.ShapeDtypeStruct((nb, SUBLANES, qkv), F32),
        ],
        scratch_shapes=[pltpu.VMEM((tt + SUBLANES, sc), F32), pltpu.VMEM((tt + SUBLANES, qkv), F32)],
        compiler_params=_params(("parallel", "arbitrary"), 48),
        name="prep",
    )(proj, proj, proj, proj, ba, scbuf, dnbuf, scw, dnw, alog, dtb)


_INV_BASE = 8

def _delta_body(q_ref, k_ref, v_ref, bg_ref, s0_ref, o_ref, sout_ref, s_ref, low_ref, qk_ref):
    c = pl.program_id(1)
    nc = pl.num_programs(1)

    @pl.when(c == 0)
    def _():
        s_ref[...] = s0_ref[...].reshape(s_ref.shape)

    ri = lax.broadcasted_iota(jnp.int32, (CHUNK, CHUNK), 0)
    ci = lax.broadcasted_iota(jnp.int32, (CHUNK, CHUNK), 1)
    causal = ri >= ci
    strict = ri > ci
    eye = (ri == ci).astype(F32)
    base = _INV_BASE
    base_blocks = (ri // base) == (ci // base)
    levels = []
    s = base
    while s < CHUNK:
        levels.append(((ri // (2 * s)) == (ci // (2 * s))) & ((ri // s) % 2 == 1) & ((ci // s) % 2 == 0))
        s *= 2

    ns = q_ref.shape[0]
    rw = lax.broadcasted_iota(jnp.int32, (CHUNK, LANES), 0)
    cw = lax.broadcasted_iota(jnp.int32, (CHUNK, LANES), 1)
    upper = jnp.concatenate([(rw <= cw).astype(F32)] * DN_HEADS, axis=1)
    spread = lambda b, l0: [jnp.broadcast_to(bg_ref[b][:, l0 + h:l0 + h + 1], (CHUNK, LANES)) for h in range(DN_HEADS)]
    beta_x = [spread(b, 0) for b in range(ns)]
    gx = [jnp.concatenate(spread(b, DN_HEADS), axis=1) for b in range(ns)]
    gc_all = [_dot_hi(causal.astype(F32), gx[b]) for b in range(ns)]
    gcrow_all = [_dot_hi(jnp.ones((CHUNK, CHUNK), F32), gx[b] * upper) for b in range(ns)]
    units = [(b, h) for b in range(ns) for h in range(DN_HEADS)]
    idx = range(len(units))
    cs = [slice(h * LANES, (h + 1) * LANES) for h in range(DN_HEADS)]
    q_of = lambda i: q_ref[units[i][0], :, cs[units[i][1]]]
    k_of = lambda i: k_ref[units[i][0], :, cs[units[i][1]]]
    v_of = lambda i: v_ref[units[i][0], :, cs[units[i][1]]]
    bx_of = lambda i: beta_x[units[i][0]][units[i][1]]
    gc_of = lambda i: gc_all[units[i][0]][:, cs[units[i][1]]]
    gcrow_of = lambda i: gcrow_all[units[i][0]][:, cs[units[i][1]]]
    kk = [_dot_nt(jnp.concatenate([k_of(i) * bx_of(i), q_of(i)], axis=0), k_of(i)) for i in idx]
    for i in idx:
        decay = jnp.exp(jnp.where(causal, gc_of(i)[:, :CHUNK] - gcrow_of(i)[:, :CHUNK], NEG_INF))
        low_ref[i] = jnp.where(strict, kk[i][:CHUNK] * decay, 0.0)
        qk_ref[i] = (kk[i][CHUNK:] * decay).astype(BF16)
    m = [jnp.where(base_blocks, -low_ref[i], 0.0) for i in idx]
    tmat = [eye + m[i] for i in idx]
    p = 1
    while 2 * p < base:
        m = [_dot3(m[i], m[i]) for i in idx]
        tm = [_dot3(tmat[i], m[i]) for i in idx]
        tmat = [tmat[i] + tm[i] for i in idx]
        p *= 2
    for lvl in levels:
        tc = [_dot3(tmat[i], jnp.where(lvl, low_ref[i], 0.0)) for i in idx]
        tct = [_dot3(tc[i], tmat[i]) for i in idx]
        tmat = [tmat[i] - tct[i] for i in idx]
    uw = []
    for i in idx:
        rhs = jnp.concatenate([v_of(i) * bx_of(i), k_of(i) * bx_of(i) * jnp.exp(gc_of(i))], axis=1)
        uw.append(_dot(tmat[i], rhs))
    ws_qs = [_dot(jnp.concatenate([uw[i][:, DN_DV:], q_of(i) * jnp.exp(gc_of(i))], axis=0), s_ref[i])
             for i in idx]
    v_new = [uw[i][:, :DN_DV] - ws_qs[i][:CHUNK] for i in idx]
    qv = [jnp.dot(qk_ref[i], v_new[i].astype(BF16), preferred_element_type=F32) for i in idx]
    for i in idx:
        o_ref[units[i][0], :, cs[units[i][1]]] = ws_qs[i][CHUNK:] + qv[i]
    for i in idx:
        gc = gc_of(i)
        g_last = gc[CHUNK - 1:CHUNK, :]
        s_ref[i] = s_ref[i] * jnp.exp(g_last) + _dot_tn(k_of(i) * jnp.exp(g_last - gc), v_new[i])

    @pl.when(c == nc - 1)
    def _():
        sout_ref[...] = s_ref[...].reshape(sout_ref.shape)


def _delta(q, k, v, bg, s0, nb, t):
    nc = t // CHUNK
    vw = DN_HEADS * DN_DV
    ns = 2 if nb % 2 == 0 else 1
    tok = pl.BlockSpec((ns, CHUNK, vw), lambda b, c: (b, c, 0))
    st = pl.BlockSpec((ns, DN_HEADS, DN_DK, DN_DV), lambda b, c: (b, 0, 0, 0))
    seq = lambda a: a.reshape(nb, t, a.shape[-1])
    o, s_new = pl.pallas_call(
        _delta_body,
        grid=(nb // ns, nc),
        in_specs=[tok, tok, tok, pl.BlockSpec((ns, CHUNK, LANES), lambda b, c: (b, c, 0)), st],
        out_specs=[tok, st],
        out_shape=[jax.ShapeDtypeStruct((nb, t, vw), F32),
                   jax.ShapeDtypeStruct((nb, DN_HEADS, DN_DK, DN_DV), F32)],
        scratch_shapes=[pltpu.VMEM((ns * DN_HEADS, DN_DK, DN_DV), F32),
                        pltpu.VMEM((ns * DN_HEADS, CHUNK, CHUNK), F32),
                        pltpu.VMEM((ns * DN_HEADS, CHUNK, CHUNK), BF16)],
        compiler_params=_params(("parallel", "arbitrary"), 32),
        name="delta",
    )(seq(q), seq(k), seq(v), seq(bg), s0)
    return o.reshape(nb * t, vw), s_new


def _merge_body(x_ref, ysc_ref, o_ref, z_ref, gsc_ref, gdn_ref, ng_ref, wsc_ref, wdn_ref, wo_ref, h_ref):
    tm = x_ref.shape[0]
    parts = []
    for h in range(DN_HEADS):
        cs = slice(h * LANES, (h + 1) * LANES)
        z = z_ref[:, cs]
        on = _rms(o_ref[:, cs], ng_ref[...]) * (z * jax.nn.sigmoid(z))
        parts.append(on.astype(BF16))
    y_dn = jnp.dot(jnp.concatenate(parts, axis=1), wdn_ref[...], preferred_element_type=F32)
    y_sc = jnp.dot(ysc_ref[...], wsc_ref[...], preferred_element_type=F32)
    merged = jax.nn.sigmoid(gsc_ref[...]) * y_sc + jax.nn.sigmoid(gdn_ref[...]) * y_dn
    h_ref[...] = x_ref[...] + jnp.dot(merged.astype(BF16), wo_ref[...], preferred_element_type=F32)


def _merge(x, ysc, o, proj, ng, wsc, wdn, wo, tm, col0_z, col0_gsc, col0_gdn):
    n, d = x.shape
    sc = ysc.shape[1]
    vw = o.shape[1]
    const = lambda shape: pl.BlockSpec(shape, lambda i: (0, 0), pipeline_mode=pl.Buffered(1))
    return pl.pallas_call(
        _merge_body,
        grid=(n // tm,),
        in_specs=[
            pl.BlockSpec((tm, d), lambda i: (i, 0)),
            pl.BlockSpec((tm, sc), lambda i: (i, 0)),
            pl.BlockSpec((tm, vw), lambda i: (i, 0)),
            pl.BlockSpec((tm, vw), lambda i: (i, col0_z // vw)),
            pl.BlockSpec((tm, d), lambda i: (i, col0_gsc // d)),
            pl.BlockSpec((tm, d), lambda i: (i, col0_gdn // d)),
            const((1, LANES)),
            const((sc, d)),
            const((vw, d)),
            const((d, d)),
        ],
        out_specs=pl.BlockSpec((tm, d), lambda i: (i, 0)),
        out_shape=jax.ShapeDtypeStruct((n, d), F32),
        compiler_params=_params(("parallel",), 48),
        name="merge",
    )(x, ysc, o, proj, proj, proj, ng, wsc, wdn, wo)


_CAND_ROWS = 10 * SUBLANES


def _tree(op, xs):
    xs = list(xs)
    while len(xs) > 1:
        xs = [op(xs[i], xs[i + 1]) for i in range(0, len(xs) - 1, 2)] + (xs[-1:] if len(xs) % 2 else [])
    return xs[0]


_NO_INDEX = float(2 ** 20)


def _any_lane(flags):
    return jnp.max(_tree(jnp.maximum, flags)) > 0.5


def _topk_ranks(scores, exact):
    nk, ln = scores[0].shape
    ng = nk // SUBLANES
    probs = range(len(scores))
    sub = lax.broadcasted_iota(jnp.int32, (SUBLANES, ln), 0).astype(F32)
    vals = [[s[g * SUBLANES:(g + 1) * SUBLANES, :] for g in range(ng)] for s in scores]
    rank = [[jnp.full((SUBLANES, ln), float(PEER_TOPK), F32) for _ in range(ng)] for _ in probs]
    base = [float(g * SUBLANES) for g in range(ng)]
    tops = [[] for _ in probs]
    for it in range(PEER_TOPK):
        m8 = [_tree(jnp.maximum, vals[p]) for p in probs]
        m = [jnp.max(m8[p], axis=0, keepdims=True) for p in probs]
        if exact:
            cand = [_tree(jnp.minimum, [jnp.where(vals[p][g] == m[p], base[g], _NO_INDEX) for g in range(ng)])
                    for p in probs]
            first = [jnp.min(cand[p] + sub, axis=0, keepdims=True) for p in probs]
            grp = [first[p] - sub for p in probs]
        for p in probs:
            tops[p].append(m[p])
            for g in range(ng):
                sel = (grp[p] == base[g]) if exact else (vals[p][g] == m[p])
                rank[p][g] = jnp.where(sel, float(it), rank[p][g])
                vals[p][g] = jnp.where(sel, NEG_INF, vals[p][g])
    out = []
    for p in probs:
        if exact:
            bad = jnp.zeros((1, ln), F32)
        else:
            ranked = _tree(jnp.add, [jnp.where(rank[p][g] < float(PEER_TOPK), 1.0, 0.0) for g in range(ng)])
            bad = jnp.where(jnp.sum(ranked, axis=0, keepdims=True) != float(PEER_TOPK), 1.0, 0.0)
        out.append((jnp.concatenate(rank[p], axis=0), jnp.concatenate(tops[p], axis=0), bad))
    return out


def _joint_select(pairs, exact):
    ln = pairs[0][0].shape[1]
    probs = range(len(pairs))
    sub = lax.broadcasted_iota(jnp.int32, (SUBLANES, ln), 0).astype(F32)
    isub = lax.broadcasted_iota(jnp.int32, (SUBLANES, ln), 0)
    pos_of = [r * PEER_TOPK + sub for r in range(SUBLANES)]
    pos_of.append(SUBLANES + sub)
    pos_of.append((SUBLANES + sub) * PEER_TOPK)
    ngr = len(pos_of)
    vals = []
    for v1, v2 in pairs:
        blocks = []
        for r in range(SUBLANES):
            cnt = min(SUBLANES, PEER_TOPK // (r + 1))
            blocks.append(jnp.where(isub < cnt, v1[r:r + 1, :] + v2[0:SUBLANES, :], NEG_INF))
        blocks.append(v1[0:1, :] + v2[SUBLANES:, :])
        blocks.append(v1[SUBLANES:, :] + v2[0:1, :])
        vals.append(blocks)
    chosen = [[jnp.zeros((SUBLANES, ln), F32) for _ in range(ngr)] for _ in probs]
    for _ in range(PEER_TOPK):
        m8 = [_tree(jnp.maximum, vals[p]) for p in probs]
        m = [jnp.max(m8[p], axis=0, keepdims=True) for p in probs]
        if exact:
            pos = [[jnp.where(vals[p][g] == m[p], pos_of[g], _NO_INDEX) for g in range(ngr)] for p in probs]
            first = [jnp.min(_tree(jnp.minimum, pos[p]), axis=0, keepdims=True) for p in probs]
        for p in probs:
            for g in range(ngr):
                sel = (pos[p][g] == first[p]) if exact else (vals[p][g] == m[p])
                chosen[p][g] = jnp.where(sel, 1.0, chosen[p][g])
                vals[p][g] = jnp.where(sel, NEG_INF, vals[p][g])
    out = []
    for p in probs:
        if exact:
            bad = jnp.zeros((1, ln), F32)
        else:
            n_chosen = jnp.sum(_tree(jnp.add, chosen[p]), axis=0, keepdims=True)
            bad = jnp.where(n_chosen != float(PEER_TOPK), 1.0, 0.0)
        out.append((jnp.concatenate(chosen[p], axis=0), bad))
    return out


def _router_body(h_ref, g_ref, wqt_ref, keys_ref, xnt_ref, a_ref, b_ref, c_ref, r_ref, qt_ref, st_ref, rk_ref, vl_ref,
                 ch_ref):
    tm = h_ref.shape[0]
    xn = _rms(h_ref[...], g_ref[...])
    xnt = xn.T.astype(BF16)
    xnt_ref[...] = xnt
    qt_ref[...] = jnp.dot(wqt_ref[...], xnt, preferred_element_type=F32).astype(BF16)

    for hp in range(2 * PEER_HEADS):
        q = qt_ref[hp * PEER_HALF:(hp + 1) * PEER_HALF, :]
        st_ref[hp] = jnp.dot(keys_ref[hp], q, preferred_element_type=F32)

    chunks = [slice(lc * LANES, (lc + 1) * LANES) for lc in range(tm // LANES)]

    def rank_step(hp, carry):
        def run(exact):
            res = _topk_ranks([st_ref[hp, :, ls] for ls in chunks], exact)
            for ls, (rank, vals, _) in zip(chunks, res):
                rk_ref[hp, :, ls] = rank
                vl_ref[hp, :, ls] = vals
            return _any_lane([bad for _, _, bad in res])

        @pl.when(run(False))
        def _():
            run(True)

        return carry

    lax.fori_loop(0, 2 * PEER_HEADS, rank_step, 0)

    def head_step(h, carry):
        tops = [(vl_ref[2 * h, :, ls], vl_ref[2 * h + 1, :, ls]) for ls in chunks]

        def run(exact):
            res = _joint_select(tops, exact)
            for ls, (chosen, _) in zip(chunks, res):
                ch_ref[:, ls] = chosen
            return _any_lane([bad for _, bad in res])

        @pl.when(run(False))
        def _():
            run(True)

        for ls, (v1, v2) in zip(chunks, tops):
            chosen = ch_ref[:, ls]
            e1 = jnp.exp(v1 - v1[0:1, :])
            e2 = jnp.exp(v2 - v2[0:1, :])
            z = jnp.zeros((1, LANES), F32)
            c_rows = []
            for r in range(SUBLANES):
                blk = chosen[r * SUBLANES:(r + 1) * SUBLANES, :]
                cnt = jnp.sum(blk, axis=0, keepdims=True)
                z = z + e1[r:r + 1, :] * jnp.sum(blk * e2[0:SUBLANES, :], axis=0, keepdims=True)
                if r == 0:
                    strip = chosen[SUBLANES * SUBLANES:SUBLANES * SUBLANES + SUBLANES, :]
                    cnt = cnt + jnp.sum(strip, axis=0, keepdims=True)
                    z = z + e1[0:1, :] * jnp.sum(strip * e2[SUBLANES:, :], axis=0, keepdims=True)
                c_rows.append(cnt)
            strip = chosen[SUBLANES * SUBLANES + SUBLANES:, :]
            z = z + e2[0:1, :] * jnp.sum(strip * e1[SUBLANES:, :], axis=0, keepdims=True)
            counts = jnp.concatenate(c_rows + [strip], axis=0)
            inv_z = 1.0 / z

            rank1 = rk_ref[2 * h, :, ls]
            rank2 = rk_ref[2 * h + 1, :, ls]
            in1 = rank1 < float(PEER_TOPK)
            in2 = rank2 < float(PEER_TOPK)
            a_key = jnp.exp(jnp.where(in1, st_ref[2 * h, :, ls] - v1[0:1, :], NEG_INF)) * inv_z
            b_key = jnp.exp(jnp.where(in2, st_ref[2 * h + 1, :, ls] - v2[0:1, :], NEG_INF))
            c_key = jnp.zeros((PEER_NKEYS, LANES), F32)
            for r in range(PEER_TOPK):
                c_key = jnp.where(rank1 == float(r), counts[r:r + 1, :], c_key)
            a_ref[h, :, ls] = a_key
            c_ref[h, :, ls] = c_key * GATE_STEP
            b_ref[h, :, ls] = b_key.astype(BF16)
            r_ref[h, :, ls] = (rank2 * GATE_STEP).astype(BF16)
        return carry

    lax.fori_loop(0, PEER_HEADS, head_step, 0)


def _router(h, g, wqt, keys, tm):
    n, d = h.shape
    qd = wqt.shape[0]
    n_hp = 2 * PEER_HEADS
    fac = lambda: pl.BlockSpec((PEER_HEADS, PEER_NKEYS, tm), lambda i: (0, 0, i))
    fac_shape = lambda dt: jax.ShapeDtypeStruct((PEER_HEADS, PEER_NKEYS, n), dt)
    return pl.pallas_call(
        _router_body,
        grid=(n // tm,),
        in_specs=[
            pl.BlockSpec((tm, d), lambda i: (i, 0)),
            pl.BlockSpec((1, d), lambda i: (0, 0)),
            pl.BlockSpec((qd, d), lambda i: (0, 0), pipeline_mode=pl.Buffered(1)),
            pl.BlockSpec((n_hp, PEER_NKEYS, PEER_HALF), lambda i: (0, 0, 0), pipeline_mode=pl.Buffered(1)),
        ],
        out_specs=[pl.BlockSpec((d, tm), lambda i: (0, i)), fac(), fac(), fac(), fac()],
        out_shape=[jax.ShapeDtypeStruct((d, n), BF16), fac_shape(F32), fac_shape(BF16), fac_shape(F32),
                   fac_shape(BF16)],
        scratch_shapes=[
            pltpu.VMEM((qd, tm), BF16),
            pltpu.VMEM((n_hp, PEER_NKEYS, tm), F32),
            pltpu.VMEM((n_hp, PEER_NKEYS, tm), F32),
            pltpu.VMEM((n_hp, PEER_TOPK, tm), F32),
            pltpu.VMEM((_CAND_ROWS, tm), F32),
        ],
        compiler_params=_params(("parallel",), 48),
        name="router",
    )(h, g, wqt, keys)


def _experts_body(xnt_ref, u_ref, vt_ref, a_ref, b_ref, c_ref, r_ref, h_ref, g_ref, y_ref, acc_ref,
                  hid0_ref, hid1_ref, w_ref, *, ne, final_norm):
    s = pl.program_id(0)
    n_i = u_ref.shape[0] // PEER_NKEYS
    tm = xnt_ref.shape[1]
    pk = b_ref.shape[2]
    zero = jnp.zeros((), BF16)

    @pl.when(s == 0)
    def _():
        acc_ref[...] = jnp.zeros_like(acc_ref)
        hid1_ref[...] = jnp.zeros_like(hid1_ref)

    def step(hid_next, hid_cur):
        hid_next[...] = jnp.dot(u_ref[...], xnt_ref[...], preferred_element_type=F32)
        for ii in range(n_i):
            gate = jnp.zeros((PEER_NKEYS // pk, pk, tm), BF16)
            for h in range(PEER_HEADS):
                a_i = jnp.broadcast_to(a_ref[h, ii:ii + 1, :], (pk, tm)).astype(BF16)
                c_i = jnp.broadcast_to(c_ref[h, ii:ii + 1, :], (pk, tm)).astype(BF16)
                gate = gate + jnp.minimum(a_i * b_ref[h], jnp.maximum(c_i - r_ref[h], zero))
            x = hid_cur[ii * PEER_NKEYS:(ii + 1) * PEER_NKEYS, :]
            act = 0.5 * x * (1.0 + lax.erf(x * (2.0 ** -0.5)))
            w_ref[ii * PEER_NKEYS:(ii + 1) * PEER_NKEYS, :] = act.astype(BF16) * gate.reshape(PEER_NKEYS, tm)
        acc_ref[...] += jnp.dot(vt_ref[...], w_ref[...], preferred_element_type=F32)

    @pl.when(s % 2 == 0)
    def _():
        step(hid0_ref, hid1_ref)

    @pl.when(s % 2 == 1)
    def _():
        step(hid1_ref, hid0_ref)

    @pl.when((s > 0) & (s % ne == 0))
    def _():
        y = h_ref[...] + acc_ref[...].T
        y_ref[...] = _rms(y, g_ref[...]) if final_norm else y
        acc_ref[...] = jnp.zeros_like(acc_ref)


def _experts(xnt, u, vt, a, b, c, r, h, g, tm, final_norm):
    d, n = xnt.shape
    ne, _, te = vt.shape
    n_i = te // PEER_NKEYS
    pk = b.shape[2]
    steps = (n // tm) * ne
    ahead = lambda s: jnp.minimum(s, steps - 1)
    cur = lambda s: jnp.maximum(s - 1, 0)
    full = lambda: pl.BlockSpec((PEER_HEADS, PEER_NKEYS // pk, pk, tm), lambda s: (0, 0, 0, cur(s) // ne))
    rows = lambda: pl.BlockSpec((PEER_HEADS, n_i, tm), lambda s: (0, cur(s) % ne, cur(s) // ne))
    return pl.pallas_call(
        functools.partial(_experts_body, ne=ne, final_norm=final_norm),
        grid=(steps + 1,),
        in_specs=[
            pl.BlockSpec((d, tm), lambda s: (0, ahead(s) // ne)),
            pl.BlockSpec((te, d), lambda s: (ahead(s) % ne, 0)),
            pl.BlockSpec((None, d, te), lambda s: (cur(s) % ne, 0, 0)),
            rows(), full(), rows(), full(),
            pl.BlockSpec((tm, d), lambda s: (cur(s) // ne, 0)),
            pl.BlockSpec((1, d), lambda s: (0, 0)),
        ],
        out_specs=pl.BlockSpec((tm, d), lambda s: (cur(s) // ne, 0)),
        out_shape=jax.ShapeDtypeStruct((n, d), F32),
        scratch_shapes=[pltpu.VMEM((d, tm), F32), pltpu.VMEM((te, tm), F32), pltpu.VMEM((te, tm), F32),
                        pltpu.VMEM((te, tm), BF16)],
        compiler_params=_params(("arbitrary",), 58),
        name="experts",
    )(xnt, u, vt, a, b, c, r, h, g)


def _pick(n, pref):
    t = min(pref, n)
    while n % t:
        t //= 2
    return t


def _pad_rows_front(a, rows):
    return jnp.pad(a, ((0, 0), (rows - a.shape[1], 0), (0, 0)))


def _lane_row(vals, offset):
    return jnp.zeros((1, LANES), F32).at[0, offset:offset + vals.shape[0]].set(vals.astype(F32))


def _layer(hs, states, wts, final_g):
    (norm1_g, w_in, sc_conv_w, dn_conv_w, a_log, dt_bias, dn_norm_g, w_br_sc, w_br_dn, w_o, norm2_g,
     peer_wq, peer_keys, peer_u, peer_v) = wts
    d = w_in.shape[0]
    sc = sc_conv_w.shape[1]
    qkv = dn_conv_w.shape[1]
    vw = DN_HEADS * DN_DV
    sc_taps, dn_taps = sc_conv_w.shape[0], dn_conv_w.shape[0]

    o_scb, o_scc, o_sch, o_qkv = 0, sc, 2 * sc, 3 * sc
    o_z = o_qkv + qkv
    o_beta = o_z + vw
    o_a = o_beta + DN_HEADS
    o_gsc = o_a + DN_HEADS
    o_gdn = o_gsc + d
    seg = lambda o, w: w_in[:, o:o + w]
    w_main = jnp.concatenate([seg(o_qkv, qkv), seg(o_z, vw), seg(o_gsc, d), seg(o_gdn, d),
                              seg(o_scb, sc), seg(o_scc, sc), seg(o_sch, sc)], axis=1).astype(BF16)
    w_main = w_main.reshape(d, w_main.shape[1] // PROJ_TILE, PROJ_TILE).transpose(1, 0, 2)
    col_z, col_gsc, col_gdn, col_sc = qkv, qkv + vw, qkv + vw + d, qkv + vw + 2 * d
    w_ba = jnp.pad(jnp.concatenate([seg(o_beta, DN_HEADS), seg(o_a, DN_HEADS)], axis=1),
                   ((0, 0), (0, LANES - 2 * DN_HEADS))).astype(BF16)
    scw = jnp.pad(sc_conv_w, ((0, SUBLANES - sc_taps), (0, 0)))
    dnw = jnp.pad(dn_conv_w, ((0, SUBLANES - dn_taps), (0, 0)))
    alog = _lane_row(a_log, DN_HEADS)
    dtb = _lane_row(dt_bias, DN_HEADS)
    ng = dn_norm_g.reshape(1, DN_DV).astype(F32)
    wsc, wdn, wo = w_br_sc.astype(BF16), w_br_dn.astype(BF16), w_o.astype(BF16)
    wqt = peer_wq.T.astype(BF16)
    keys = peer_keys.reshape(2 * PEER_HEADS, PEER_NKEYS, PEER_HALF).astype(BF16)
    u_tab = peer_u.astype(BF16)
    vt_tab = peer_v.astype(BF16).reshape(-1, EXPERT_TILE, d).transpose(0, 2, 1)
    g1 = norm1_g.reshape(1, d)
    g2 = norm2_g.reshape(1, d)
    gf = jnp.ones((1, d), F32) if final_g is None else final_g.reshape(1, d)

    outs, new_states = [], []
    for x3, (dn_state, dn_buf, sc_buf) in zip(hs, states):
        nb, t, _ = x3.shape
        n = nb * t
        x = x3.reshape(n, d)
        proj, ba = _inproj(x, g1, w_main, w_ba, _pick(n, 1024))
        ysc, q, k, v, bg, sc_new, dn_new = _prep(
            proj, ba, _pad_rows_front(sc_buf, SUBLANES), _pad_rows_front(dn_buf, SUBLANES), scw, dnw, alog, dtb,
            nb, t, _pick(t, 256), sc, qkv, col_sc, sc_taps, dn_taps)
        o, s_new = _delta(q, k, v, bg, dn_state.astype(F32), nb, t)
        h1 = _merge(x, ysc, o, proj, ng, wsc, wdn, wo, _pick(n, 256), col_z, col_gsc, col_gdn)
        xnt, fa, fb, fc, fr = _router(h1, g2, wqt, keys, _pick(n, 256))
        pack = lambda f: f.reshape(PEER_HEADS, PEER_NKEYS // BF16_ROWS, BF16_ROWS, n)
        y = _experts(xnt, u_tab, vt_tab, fa, pack(fb), fc, pack(fr), h1, gf, _pick(n, 512), final_g is not None)
        outs.append(y.reshape(nb, t, d))
        new_states.append((s_new.astype(dn_state.dtype), dn_new[:, SUBLANES - (dn_taps - 1):, :],
                           sc_new[:, SUBLANES - (sc_taps - 1):, :]))
    return outs, new_states


def kernel(x_prompt, x_sample, state_dn, state_dn_conv, state_sc_conv, norm1_g, w_in, sc_conv_w, dn_conv_w,
           dn_a_log, dn_dt_bias, dn_norm_g, w_br_sc, w_br_dn, w_o, norm2_g, peer_wq, peer_keys, peer_u, peer_v,
           final_norm_g):
    depth = w_in.shape[0]
    nbp = x_prompt.shape[0]
    hs = [x_prompt, x_sample]
    per_layer = []
    for l in range(depth):
        wts = (norm1_g[l], w_in[l], sc_conv_w[l], dn_conv_w[l], dn_a_log[l], dn_dt_bias[l], dn_norm_g[l],
               w_br_sc[l], w_br_dn[l], w_o[l], norm2_g[l], peer_wq[l], peer_keys[l], peer_u[l], peer_v[l])
        zero_states = (jnp.zeros((nbp, DN_HEADS, DN_DK, DN_DV), state_dn.dtype),
                       jnp.zeros((nbp, dn_conv_w.shape[1] - 1, dn_conv_w.shape[2]), x_prompt.dtype),
                       jnp.zeros((nbp, sc_conv_w.shape[1] - 1, sc_conv_w.shape[2]), x_prompt.dtype))
        states = [zero_states, (state_dn[l], state_dn_conv[l], state_sc_conv[l])]
        hs, new_states = _layer(hs, states, wts, final_norm_g if l == depth - 1 else None)
        per_layer.append(new_states)
    stack = lambda g, i: jnp.stack([per_layer[l][g][i] for l in range(depth)])
    return (hs[0], hs[1], stack(0, 0), stack(0, 1), stack(0, 2), stack(1, 0), stack(1, 1), stack(1, 2))
```

```python
import functools
import math

import jax
import jax.numpy as jnp
from jax import lax
from jax.experimental import pallas as pl
from jax.experimental.pallas import tpu as pltpu

F32 = jnp.float32
BF16 = jnp.bfloat16
EPS = 1e-6
LANES = 128
SUBLANES = 8
BF16_ROWS = 16
GATE_STEP = 2.0
PROJ_TILE = 1024
EXPERT_TILE = 1024
CHUNK = 64
DN_HEADS = 8
DN_DK = 128
DN_DV = 128
PEER_HEADS = 8
PEER_NKEYS = 128
PEER_HALF = 128
PEER_TOPK = 16
NEG_INF = float("-inf")
MIB = 2 ** 20


def _params(sem, vmem_mib):
    return pltpu.CompilerParams(dimension_semantics=sem, vmem_limit_bytes=vmem_mib * MIB)


def _dot(a, b):
    return jnp.dot(a.astype(BF16), b.astype(BF16), preferred_element_type=F32)


def _dot_nt(a, b):
    return lax.dot_general(a.astype(BF16), b.astype(BF16), (((1,), (1,)), ((), ())),
                           preferred_element_type=F32)


def _dot_tn(a, b):
    return lax.dot_general(a.astype(BF16), b.astype(BF16), (((0,), (0,)), ((), ())),
                           preferred_element_type=F32)


def _split(x):
    hi = x.astype(BF16)
    return hi, (x - hi.astype(F32)).astype(BF16)


def _dot3(a, b):
    ah, al = _split(a)
    bh, bl = _split(b)
    return jnp.dot(jnp.concatenate([ah, al, ah], axis=1), jnp.concatenate([bh, bh, bl], axis=0),
                   preferred_element_type=F32)


def _rms(x, g):
    return x * lax.rsqrt(jnp.mean(x * x, axis=-1, keepdims=True) + EPS) * g


def _softplus(x):
    return jnp.maximum(x, 0.0) + jnp.log1p(jnp.exp(-jnp.abs(x)))


def _inproj_body(x_ref, g_ref, w_ref, wba_ref, proj_ref, ba_ref, xn_ref):
    @pl.when(pl.program_id(1) == 0)
    def _():
        xn = _rms(x_ref[...], g_ref[...]).astype(BF16)
        xn_ref[...] = xn
        ba_ref[...] = jnp.dot(xn, wba_ref[...], preferred_element_type=F32)

    proj_ref[...] = jnp.dot(xn_ref[...], w_ref[...], preferred_element_type=F32)


def _inproj(x, g, w_main, w_ba, tm):
    n, d = x.shape
    n_tiles, _, tn = w_main.shape
    width = n_tiles * tn
    return pl.pallas_call(
        _inproj_body,
        grid=(n // tm, n_tiles),
        in_specs=[
            pl.BlockSpec((tm, d), lambda i, j: (i, 0)),
            pl.BlockSpec((1, d), lambda i, j: (0, 0)),
            pl.BlockSpec((None, d, tn), lambda i, j: (j, 0, 0)),
            pl.BlockSpec((d, LANES), lambda i, j: (0, 0)),
        ],
        out_specs=[
            pl.BlockSpec((tm, tn), lambda i, j: (i, j)),
            pl.BlockSpec((tm, LANES), lambda i, j: (i, 0)),
        ],
        out_shape=[jax.ShapeDtypeStruct((n, width), F32), jax.ShapeDtypeStruct((n, LANES), F32)],
        scratch_shapes=[pltpu.VMEM((tm, d), BF16)],
        compiler_params=_params(("parallel", "arbitrary"), 48),
        name="inproj",
    )(x, g, w_main, w_ba)


def _prep_body(scb_ref, scc_ref, sch_ref, qkv_ref, ba_ref, scbuf_ref, dnbuf_ref, scw_ref, dnw_ref,
               alog_ref, dtb_ref,
               ysc_ref, q_ref, k_ref, v_ref, bg_ref, scnew_ref, dnnew_ref,
               ext_sc, ext_dn, *, sc_taps, dn_taps):
    tt = scb_ref.shape[0]
    halo = SUBLANES

    @pl.when(pl.program_id(1) == 0)
    def _():
        ext_sc[0:halo, :] = scbuf_ref[...]
        ext_dn[0:halo, :] = dnbuf_ref[...]

    ext_sc[halo:halo + tt, :] = scc_ref[...] * sch_ref[...]
    for cg in range(scb_ref.shape[1] // LANES):
        cs = slice(cg * LANES, (cg + 1) * LANES)
        acc = None
        for i in range(sc_taps):
            r0 = halo - (sc_taps - 1) + i
            term = ext_sc[r0:r0 + tt, cs] * scw_ref[i:i + 1, cs]
            acc = term if acc is None else acc + term
        ysc_ref[:, cs] = (scb_ref[:, cs] * acc).astype(BF16)
    scnew_ref[...] = ext_sc[tt:tt + halo, :]
    ext_sc[0:halo, :] = ext_sc[tt:tt + halo, :]

    ext_dn[halo:halo + tt, :] = qkv_ref[...]
    n_qk = DN_HEADS * DN_DK // LANES
    for cg in range(qkv_ref.shape[1] // LANES):
        cs = slice(cg * LANES, (cg + 1) * LANES)
        acc = None
        for i in range(dn_taps):
            r0 = halo - (dn_taps - 1) + i
            term = ext_dn[r0:r0 + tt, cs] * dnw_ref[i:i + 1, cs]
            acc = term if acc is None else acc + term
        a = acc * jax.nn.sigmoid(acc)
        if cg < 2 * n_qk:
            a = a * lax.rsqrt(jnp.sum(a * a, axis=-1, keepdims=True) + EPS)
        if cg < n_qk:
            q_ref[:, cs] = a * (DN_DK ** -0.5)
        elif cg < 2 * n_qk:
            k_ref[:, slice((cg - n_qk) * LANES, (cg - n_qk + 1) * LANES)] = a
        else:
            v_ref[:, slice((cg - 2 * n_qk) * LANES, (cg - 2 * n_qk + 1) * LANES)] = a
    dnnew_ref[...] = ext_dn[tt:tt + halo, :]
    ext_dn[0:halo, :] = ext_dn[tt:tt + halo, :]

    ba = ba_ref[...]
    lane = lax.broadcasted_iota(jnp.int32, ba.shape, 1)
    bg_ref[...] = jnp.where(lane < DN_HEADS, jax.nn.sigmoid(ba),
                            -jnp.exp(alog_ref[...]) * _softplus(ba + dtb_ref[...]))


def _prep(proj, ba, scbuf, dnbuf, scw, dnw, alog, dtb, nb, t, tt, sc, qkv, col0_sc, sc_taps, dn_taps):
    n = nb * t
    nt = t // tt
    vw = DN_HEADS * DN_DV
    row = lambda b, s: b * nt + s
    scblk = col0_sc // sc
    body = functools.partial(_prep_body, sc_taps=sc_taps, dn_taps=dn_taps)
    return pl.pallas_call(
        body,
        grid=(nb, nt),
        in_specs=[
            pl.BlockSpec((tt, sc), lambda b, s: (row(b, s), scblk)),
            pl.BlockSpec((tt, sc), lambda b, s: (row(b, s), scblk + 1)),
            pl.BlockSpec((tt, sc), lambda b, s: (row(b, s), scblk + 2)),
            pl.BlockSpec((tt, qkv), lambda b, s: (row(b, s), 0)),
            pl.BlockSpec((tt, LANES), lambda b, s: (row(b, s), 0)),
            pl.BlockSpec((None, SUBLANES, sc), lambda b, s: (b, 0, 0)),
            pl.BlockSpec((None, SUBLANES, qkv), lambda b, s: (b, 0, 0)),
            pl.BlockSpec((SUBLANES, sc), lambda b, s: (0, 0)),
            pl.BlockSpec((SUBLANES, qkv), lambda b, s: (0, 0)),
            pl.BlockSpec((1, LANES), lambda b, s: (0, 0)),
            pl.BlockSpec((1, LANES), lambda b, s: (0, 0)),
        ],
        out_specs=[
            pl.BlockSpec((tt, sc), lambda b, s: (row(b, s), 0)),
            pl.BlockSpec((tt, vw), lambda b, s: (row(b, s), 0)),
            pl.BlockSpec((tt, vw), lambda b, s: (row(b, s), 0)),
            pl.BlockSpec((tt, vw), lambda b, s: (row(b, s), 0)),
            pl.BlockSpec((tt, LANES), lambda b, s: (row(b, s), 0)),
            pl.BlockSpec((None, SUBLANES, sc), lambda b, s: (b, 0, 0)),
            pl.BlockSpec((None, SUBLANES, qkv), lambda b, s: (b, 0, 0)),
        ],
        out_shape=[
            jax.ShapeDtypeStruct((n, sc), BF16),
            jax.ShapeDtypeStruct((n, vw), F32),
            jax.ShapeDtypeStruct((n, vw), F32),
            jax.ShapeDtypeStruct((n, vw), F32),
            jax.ShapeDtypeStruct((n, LANES), F32),
            jax.ShapeDtypeStruct((nb, SUBLANES, sc), F32),
            jax.ShapeDtypeStruct((nb, SUBLANES, qkv), F32),
        ],
        scratch_shapes=[pltpu.VMEM((tt + SUBLANES, sc), F32), pltpu.VMEM((tt + SUBLANES, qkv), F32)],
        compiler_params=_params(("parallel", "arbitrary"), 48),
        name="prep",
    )(proj, proj, proj, proj, ba, scbuf, dnbuf, scw, dnw, alog, dtb)


_INV_BASE = 8

def _delta_body(q_ref, k_ref, v_ref, bg_ref, s0_ref, o_ref, sout_ref, s_ref, low_ref, qk_ref):
    c = pl.program_id(1)
    nc = pl.num_programs(1)

    @pl.when(c == 0)
    def _():
        s_ref[...] = s0_ref[...].reshape(s_ref.shape)

    ri = lax.broadcasted_iota(jnp.int32, (CHUNK, CHUNK), 0)
    ci = lax.broadcasted_iota(jnp.int32, (CHUNK, CHUNK), 1)
    causal = ri >= ci
    strict = ri > ci
    eye = (ri == ci).astype(F32)
    base = _INV_BASE
    base_blocks = (ri // base) == (ci // base)
    levels = []
    s = base
    while s < CHUNK:
        levels.append(((ri // (2 * s)) == (ci // (2 * s))) & ((ri // s) % 2 == 1) & ((ci // s) % 2 == 0))
        s *= 2

    ns = q_ref.shape[0]
    rw = lax.broadcasted_iota(jnp.int32, (CHUNK, LANES), 0)
    spread = lambda a, l0: [jnp.broadcast_to(a[:, l0 + h:l0 + h + 1], (CHUNK, LANES)) for h in range(DN_HEADS)]
    beta_x, gc_col, gc_row = [], [], []
    for b in range(ns):
        run = bg_ref[b]
        shift = 1
        while shift < CHUNK:
            run = run + jnp.where(rw >= shift, pltpu.roll(run, shift, 0), 0.0)
            shift *= 2
        beta_x.append(spread(bg_ref[b], 0))
        gc_col.append(spread(run, DN_HEADS))
        run_t = jnp.concatenate([run, jnp.zeros((LANES - CHUNK, LANES), F32)], axis=0).T
        gc_row.append([jnp.broadcast_to(run_t[DN_HEADS + h:DN_HEADS + h + 1, :CHUNK], (CHUNK, CHUNK))
                       for h in range(DN_HEADS)])
    units = [(b, h) for b in range(ns) for h in range(DN_HEADS)]
    idx = range(len(units))
    cs = [slice(h * LANES, (h + 1) * LANES) for h in range(DN_HEADS)]
    q_of = lambda i: q_ref[units[i][0], :, cs[units[i][1]]]
    k_of = lambda i: k_ref[units[i][0], :, cs[units[i][1]]]
    v_of = lambda i: v_ref[units[i][0], :, cs[units[i][1]]]
    bx_of = lambda i: beta_x[units[i][0]][units[i][1]]
    gc_of = lambda i: gc_col[units[i][0]][units[i][1]]
    gcrow_of = lambda i: gc_row[units[i][0]][units[i][1]]
    kk = [_dot_nt(jnp.concatenate([k_of(i) * bx_of(i), q_of(i)], axis=0), k_of(i)) for i in idx]
    for i in idx:
        decay = jnp.exp(jnp.where(causal, gc_of(i)[:, :CHUNK] - gcrow_of(i), NEG_INF))
        low_ref[i] = jnp.where(strict, kk[i][:CHUNK] * decay, 0.0)
        qk_ref[i] = (kk[i][CHUNK:] * decay).astype(BF16)
    m = [jnp.where(base_blocks, -low_ref[i], 0.0) for i in idx]
    tmat = [eye + m[i] for i in idx]
    p = 1
    while 2 * p < base:
        m = [_dot3(m[i], m[i]) for i in idx]
        tm = [_dot3(tmat[i], m[i]) for i in idx]
        tmat = [tmat[i] + tm[i] for i in idx]
        p *= 2
    for lvl in levels:
        tc = [_dot3(tmat[i], jnp.where(lvl, low_ref[i], 0.0)) for i in idx]
        tct = [_dot3(tc[i], tmat[i]) for i in idx]
        tmat = [tmat[i] - tct[i] for i in idx]
    uw = []
    for i in idx:
        rhs = jnp.concatenate([v_of(i) * bx_of(i), k_of(i) * bx_of(i) * jnp.exp(gc_of(i))], axis=1)
        uw.append(_dot(tmat[i], rhs))
    ws_qs = [_dot(jnp.concatenate([uw[i][:, DN_DV:], q_of(i) * jnp.exp(gc_of(i))], axis=0), s_ref[i])
             for i in idx]
    v_new = [uw[i][:, :DN_DV] - ws_qs[i][:CHUNK] for i in idx]
    qv = [jnp.dot(qk_ref[i], v_new[i].astype(BF16), preferred_element_type=F32) for i in idx]
    for i in idx:
        o_ref[units[i][0], :, cs[units[i][1]]] = ws_qs[i][CHUNK:] + qv[i]
    for i in idx:
        gc = gc_of(i)
        g_last = gc[CHUNK - 1:CHUNK, :]
        s_ref[i] = s_ref[i] * jnp.exp(g_last) + _dot_tn(k_of(i) * jnp.exp(g_last - gc), v_new[i])

    @pl.when(c == nc - 1)
    def _():
        sout_ref[...] = s_ref[...].reshape(sout_ref.shape)


def _delta(q, k, v, bg, s0, nb, t):
    nc = t // CHUNK
    vw = DN_HEADS * DN_DV
    ns = 2 if nb % 2 == 0 else 1
    tok = pl.BlockSpec((ns, CHUNK, vw), lambda b, c: (b, c, 0))
    st = pl.BlockSpec((ns, DN_HEADS, DN_DK, DN_DV), lambda b, c: (b, 0, 0, 0))
    seq = lambda a: a.reshape(nb, t, a.shape[-1])
    o, s_new = pl.pallas_call(
        _delta_body,
        grid=(nb // ns, nc),
        in_specs=[tok, tok, tok, pl.BlockSpec((ns, CHUNK, LANES), lambda b, c: (b, c, 0)), st],
        out_specs=[tok, st],
        out_shape=[jax.ShapeDtypeStruct((nb, t, vw), F32),
                   jax.ShapeDtypeStruct((nb, DN_HEADS, DN_DK, DN_DV), F32)],
        scratch_shapes=[pltpu.VMEM((ns * DN_HEADS, DN_DK, DN_DV), F32),
                        pltpu.VMEM((ns * DN_HEADS, CHUNK, CHUNK), F32),
                        pltpu.VMEM((ns * DN_HEADS, CHUNK, CHUNK), BF16)],
        compiler_params=_params(("parallel", "arbitrary"), 32),
        name="delta",
    )(seq(q), seq(k), seq(v), seq(bg), s0)
    return o.reshape(nb * t, vw), s_new


def _merge_body(x_ref, ysc_ref, o_ref, z_ref, gsc_ref, gdn_ref, ng_ref, wsc_ref, wdn_ref, wo_ref, h_ref):
    tm = x_ref.shape[0]
    parts = []
    for h in range(DN_HEADS):
        cs = slice(h * LANES, (h + 1) * LANES)
        z = z_ref[:, cs]
        on = _rms(o_ref[:, cs], ng_ref[...]) * (z * jax.nn.sigmoid(z))
        parts.append(on.astype(BF16))
    y_dn = jnp.dot(jnp.concatenate(parts, axis=1), wdn_ref[...], preferred_element_type=F32)
    y_sc = jnp.dot(ysc_ref[...], wsc_ref[...], preferred_element_type=F32)
    merged = jax.nn.sigmoid(gsc_ref[...]) * y_sc + jax.nn.sigmoid(gdn_ref[...]) * y_dn
    h_ref[...] = x_ref[...] + jnp.dot(merged.astype(BF16), wo_ref[...], preferred_element_type=F32)


def _merge(x, ysc, o, proj, ng, wsc, wdn, wo, tm, col0_z, col0_gsc, col0_gdn):
    n, d = x.shape
    sc = ysc.shape[1]
    vw = o.shape[1]
    const = lambda shape: pl.BlockSpec(shape, lambda i: (0, 0), pipeline_mode=pl.Buffered(1))
    return pl.pallas_call(
        _merge_body,
        grid=(n // tm,),
        in_specs=[
            pl.BlockSpec((tm, d), lambda i: (i, 0)),
            pl.BlockSpec((tm, sc), lambda i: (i, 0)),
            pl.BlockSpec((tm, vw), lambda i: (i, 0)),
            pl.BlockSpec((tm, vw), lambda i: (i, col0_z // vw)),
            pl.BlockSpec((tm, d), lambda i: (i, col0_gsc // d)),
            pl.BlockSpec((tm, d), lambda i: (i, col0_gdn // d)),
            const((1, LANES)),
            const((sc, d)),
            const((vw, d)),
            const((d, d)),
        ],
        out_specs=pl.BlockSpec((tm, d), lambda i: (i, 0)),
        out_shape=jax.ShapeDtypeStruct((n, d), F32),
        compiler_params=_params(("parallel",), 48),
        name="merge",
    )(x, ysc, o, proj, proj, proj, ng, wsc, wdn, wo)


_CAND_ROWS = 10 * SUBLANES


def _tree(op, xs):
    xs = list(xs)
    while len(xs) > 1:
        xs = [op(xs[i], xs[i + 1]) for i in range(0, len(xs) - 1, 2)] + (xs[-1:] if len(xs) % 2 else [])
    return xs[0]


_NO_INDEX = float(2 ** 20)


def _any_lane(flags):
    return jnp.max(_tree(jnp.maximum, flags)) > 0.5


def _topk_ranks(scores, exact):
    nk, ln = scores[0].shape
    ng = nk // SUBLANES
    probs = range(len(scores))
    sub = lax.broadcasted_iota(jnp.int32, (SUBLANES, ln), 0).astype(F32)
    vals = [[s[g * SUBLANES:(g + 1) * SUBLANES, :] for g in range(ng)] for s in scores]
    rank = [[jnp.full((SUBLANES, ln), float(PEER_TOPK), F32) for _ in range(ng)] for _ in probs]
    base = [float(g * SUBLANES) for g in range(ng)]
    tops = [[] for _ in probs]
    for it in range(PEER_TOPK):
        m8 = [_tree(jnp.maximum, vals[p]) for p in probs]
        m = [jnp.max(m8[p], axis=0, keepdims=True) for p in probs]
        if exact:
            cand = [_tree(jnp.minimum, [jnp.where(vals[p][g] == m[p], base[g], _NO_INDEX) for g in range(ng)])
                    for p in probs]
            first = [jnp.min(cand[p] + sub, axis=0, keepdims=True) for p in probs]
            grp = [first[p] - sub for p in probs]
        for p in probs:
            tops[p].append(m[p])
            for g in range(ng):
                sel = (grp[p] == base[g]) if exact else (vals[p][g] == m[p])
                rank[p][g] = jnp.where(sel, float(it), rank[p][g])
                vals[p][g] = jnp.where(sel, NEG_INF, vals[p][g])
    out = []
    for p in probs:
        if exact:
            bad = jnp.zeros((1, ln), F32)
        else:
            ranked = _tree(jnp.add, [jnp.where(rank[p][g] < float(PEER_TOPK), 1.0, 0.0) for g in range(ng)])
            bad = jnp.where(jnp.sum(ranked, axis=0, keepdims=True) != float(PEER_TOPK), 1.0, 0.0)
        out.append((jnp.concatenate(rank[p], axis=0), jnp.concatenate(tops[p], axis=0), bad))
    return out


def _joint_select(pairs, exact):
    ln = pairs[0][0].shape[1]
    probs = range(len(pairs))
    sub = lax.broadcasted_iota(jnp.int32, (SUBLANES, ln), 0).astype(F32)
    isub = lax.broadcasted_iota(jnp.int32, (SUBLANES, ln), 0)
    pos_of = [r * PEER_TOPK + sub for r in range(SUBLANES)]
    pos_of.append(SUBLANES + sub)
    pos_of.append((SUBLANES + sub) * PEER_TOPK)
    ngr = len(pos_of)
    vals = []
    for v1, v2 in pairs:
        blocks = []
        for r in range(SUBLANES):
            cnt = min(SUBLANES, PEER_TOPK // (r + 1))
            blocks.append(jnp.where(isub < cnt, v1[r:r + 1, :] + v2[0:SUBLANES, :], NEG_INF))
        blocks.append(v1[0:1, :] + v2[SUBLANES:, :])
        blocks.append(v1[SUBLANES:, :] + v2[0:1, :])
        vals.append(blocks)
    chosen = [[jnp.zeros((SUBLANES, ln), F32) for _ in range(ngr)] for _ in probs]
    for _ in range(PEER_TOPK):
        m8 = [_tree(jnp.maximum, vals[p]) for p in probs]
        m = [jnp.max(m8[p], axis=0, keepdims=True) for p in probs]
        if exact:
            pos = [[jnp.where(vals[p][g] == m[p], pos_of[g], _NO_INDEX) for g in range(ngr)] for p in probs]
            first = [jnp.min(_tree(jnp.minimum, pos[p]), axis=0, keepdims=True) for p in probs]
        for p in probs:
            for g in range(ngr):
                sel = (pos[p][g] == first[p]) if exact else (vals[p][g] == m[p])
                chosen[p][g] = jnp.where(sel, 1.0, chosen[p][g])
                vals[p][g] = jnp.where(sel, NEG_INF, vals[p][g])
    out = []
    for p in probs:
        if exact:
            bad = jnp.zeros((1, ln), F32)
        else:
            n_chosen = jnp.sum(_tree(jnp.add, chosen[p]), axis=0, keepdims=True)
            bad = jnp.where(n_chosen != float(PEER_TOPK), 1.0, 0.0)
        out.append((jnp.concatenate(chosen[p], axis=0), bad))
    return out


def _router_body(h_ref, g_ref, wqt_ref, keys_ref, xnt_ref, a_ref, b_ref, c_ref, r_ref, qt_ref, st_ref, rk_ref, vl_ref,
                 ch_ref):
    tm = h_ref.shape[0]
    xn = _rms(h_ref[...], g_ref[...])
    xnt = xn.T.astype(BF16)
    xnt_ref[...] = xnt
    qt_ref[...] = jnp.dot(wqt_ref[...], xnt, preferred_element_type=F32).astype(BF16)

    for hp in range(2 * PEER_HEADS):
        q = qt_ref[hp * PEER_HALF:(hp + 1) * PEER_HALF, :]
        st_ref[hp] = jnp.dot(keys_ref[hp], q, preferred_element_type=F32)

    chunks = [slice(lc * LANES, (lc + 1) * LANES) for lc in range(tm // LANES)]

    def rank_step(hp, carry):
        def run(exact):
            res = _topk_ranks([st_ref[hp, :, ls] for ls in chunks], exact)
            for ls, (rank, vals, _) in zip(chunks, res):
                rk_ref[hp, :, ls] = rank
                vl_ref[hp, :, ls] = vals
            return _any_lane([bad for _, _, bad in res])

        @pl.when(run(False))
        def _():
            run(True)

        return carry

    lax.fori_loop(0, 2 * PEER_HEADS, rank_step, 0)

    def head_step(h, carry):
        tops = [(vl_ref[2 * h, :, ls], vl_ref[2 * h + 1, :, ls]) for ls in chunks]

        def run(exact):
            res = _joint_select(tops, exact)
            for ls, (chosen, _) in zip(chunks, res):
                ch_ref[:, ls] = chosen
            return _any_lane([bad for _, bad in res])

        @pl.when(run(False))
        def _():
            run(True)

        for ls, (v1, v2) in zip(chunks, tops):
            chosen = ch_ref[:, ls]
            e1 = jnp.exp(v1 - v1[0:1, :])
            e2 = jnp.exp(v2 - v2[0:1, :])
            z = jnp.zeros((1, LANES), F32)
            c_rows = []
            for r in range(SUBLANES):
                blk = chosen[r * SUBLANES:(r + 1) * SUBLANES, :]
                cnt = jnp.sum(blk, axis=0, keepdims=True)
                z = z + e1[r:r + 1, :] * jnp.sum(blk * e2[0:SUBLANES, :], axis=0, keepdims=True)
                if r == 0:
                    strip = chosen[SUBLANES * SUBLANES:SUBLANES * SUBLANES + SUBLANES, :]
                    cnt = cnt + jnp.sum(strip, axis=0, keepdims=True)
                    z = z + e1[0:1, :] * jnp.sum(strip * e2[SUBLANES:, :], axis=0, keepdims=True)
                c_rows.append(cnt)
            strip = chosen[SUBLANES * SUBLANES + SUBLANES:, :]
            z = z + e2[0:1, :] * jnp.sum(strip * e1[SUBLANES:, :], axis=0, keepdims=True)
            counts = jnp.concatenate(c_rows + [strip], axis=0)
            inv_z = 1.0 / z

            rank1 = rk_ref[2 * h, :, ls]
            rank2 = rk_ref[2 * h + 1, :, ls]
            in1 = rank1 < float(PEER_TOPK)
            in2 = rank2 < float(PEER_TOPK)
            a_key = jnp.exp(jnp.where(in1, st_ref[2 * h, :, ls] - v1[0:1, :], NEG_INF)) * inv_z
            b_key = jnp.exp(jnp.where(in2, st_ref[2 * h + 1, :, ls] - v2[0:1, :], NEG_INF))
            c_key = jnp.zeros((PEER_NKEYS, LANES), F32)
            for r in range(PEER_TOPK):
                c_key = jnp.where(rank1 == float(r), counts[r:r + 1, :], c_key)
            a_ref[h, :, ls] = a_key
            c_ref[h, :, ls] = c_key * GATE_STEP
            b_ref[h, :, ls] = b_key.astype(BF16)
            r_ref[h, :, ls] = (rank2 * GATE_STEP).astype(BF16)
        return carry

    lax.fori_loop(0, PEER_HEADS, head_step, 0)


def _router(h, g, wqt, keys, tm):
    n, d = h.shape
    qd = wqt.shape[0]
    n_hp = 2 * PEER_HEADS
    fac = lambda: pl.BlockSpec((PEER_HEADS, PEER_NKEYS, tm), lambda i: (0, 0, i))
    fac_shape = lambda dt: jax.ShapeDtypeStruct((PEER_HEADS, PEER_NKEYS, n), dt)
    return pl.pallas_call(
        _router_body,
        grid=(n // tm,),
        in_specs=[
            pl.BlockSpec((tm, d), lambda i: (i, 0)),
            pl.BlockSpec((1, d), lambda i: (0, 0)),
            pl.BlockSpec((qd, d), lambda i: (0, 0), pipeline_mode=pl.Buffered(1)),
            pl.BlockSpec((n_hp, PEER_NKEYS, PEER_HALF), lambda i: (0, 0, 0), pipeline_mode=pl.Buffered(1)),
        ],
        out_specs=[pl.BlockSpec((d, tm), lambda i: (0, i)), fac(), fac(), fac(), fac()],
        out_shape=[jax.ShapeDtypeStruct((d, n), BF16), fac_shape(F32), fac_shape(BF16), fac_shape(F32),
                   fac_shape(BF16)],
        scratch_shapes=[
            pltpu.VMEM((qd, tm), BF16),
            pltpu.VMEM((n_hp, PEER_NKEYS, tm), F32),
            pltpu.VMEM((n_hp, PEER_NKEYS, tm), F32),
            pltpu.VMEM((n_hp, PEER_TOPK, tm), F32),
            pltpu.VMEM((_CAND_ROWS, tm), F32),
        ],
        compiler_params=_params(("parallel",), 48),
        name="router",
    )(h, g, wqt, keys)


def _experts_body(xnt_ref, u_ref, vt_ref, a_ref, b_ref, c_ref, r_ref, h_ref, g_ref, y_ref, acc_ref,
                  hid0_ref, hid1_ref, w_ref, *, ne, final_norm):
    s = pl.program_id(0)
    n_i = u_ref.shape[0] // PEER_NKEYS
    tm = xnt_ref.shape[1]
    pk = b_ref.shape[2]
    zero = jnp.zeros((), BF16)

    @pl.when(s == 0)
    def _():
        acc_ref[...] = jnp.zeros_like(acc_ref)
        hid1_ref[...] = jnp.zeros_like(hid1_ref)

    def step(hid_next, hid_cur):
        hid_next[...] = jnp.dot(u_ref[...], xnt_ref[...], preferred_element_type=F32)
        for ii in range(n_i):
            gate = jnp.zeros((PEER_NKEYS // pk, pk, tm), BF16)
            for h in range(PEER_HEADS):
                a_i = jnp.broadcast_to(a_ref[h, ii:ii + 1, :], (pk, tm)).astype(BF16)
                c_i = jnp.broadcast_to(c_ref[h, ii:ii + 1, :], (pk, tm)).astype(BF16)
                gate = gate + jnp.minimum(a_i * b_ref[h], jnp.maximum(c_i - r_ref[h], zero))
            x = hid_cur[ii * PEER_NKEYS:(ii + 1) * PEER_NKEYS, :]
            act = 0.5 * x * (1.0 + lax.erf(x * (2.0 ** -0.5)))
            w_ref[ii * PEER_NKEYS:(ii + 1) * PEER_NKEYS, :] = act.astype(BF16) * gate.reshape(PEER_NKEYS, tm)
        acc_ref[...] += jnp.dot(vt_ref[...], w_ref[...], preferred_element_type=F32)

    @pl.when(s % 2 == 0)
    def _():
        step(hid0_ref, hid1_ref)

    @pl.when(s % 2 == 1)
    def _():
        step(hid1_ref, hid0_ref)

    @pl.when((s > 0) & (s % ne == 0))
    def _():
        y = h_ref[...] + acc_ref[...].T
        y_ref[...] = _rms(y, g_ref[...]) if final_norm else y
        acc_ref[...] = jnp.zeros_like(acc_ref)


def _experts(xnt, u, vt, a, b, c, r, h, g, tm, final_norm):
    d, n = xnt.shape
    ne, _, te = vt.shape
    n_i = te // PEER_NKEYS
    pk = b.shape[2]
    steps = (n // tm) * ne
    ahead = lambda s: jnp.minimum(s, steps - 1)
    cur = lambda s: jnp.maximum(s - 1, 0)
    full = lambda: pl.BlockSpec((PEER_HEADS, PEER_NKEYS // pk, pk, tm), lambda s: (0, 0, 0, cur(s) // ne))
    rows = lambda: pl.BlockSpec((PEER_HEADS, n_i, tm), lambda s: (0, cur(s) % ne, cur(s) // ne))
    return pl.pallas_call(
        functools.partial(_experts_body, ne=ne, final_norm=final_norm),
        grid=(steps + 1,),
        in_specs=[
            pl.BlockSpec((d, tm), lambda s: (0, ahead(s) // ne)),
            pl.BlockSpec((te, d), lambda s: (ahead(s) % ne, 0)),
            pl.BlockSpec((None, d, te), lambda s: (cur(s) % ne, 0, 0)),
            rows(), full(), rows(), full(),
            pl.BlockSpec((tm, d), lambda s: (cur(s) // ne, 0)),
            pl.BlockSpec((1, d), lambda s: (0, 0)),
        ],
        out_specs=pl.BlockSpec((tm, d), lambda s: (cur(s) // ne, 0)),
        out_shape=jax.ShapeDtypeStruct((n, d), F32),
        scratch_shapes=[pltpu.VMEM((d, tm), F32), pltpu.VMEM((te, tm), F32), pltpu.VMEM((te, tm), F32),
                        pltpu.VMEM((te, tm), BF16)],
        compiler_params=_params(("arbitrary",), 58),
        name="experts",
    )(xnt, u, vt, a, b, c, r, h, g)


def _pick(n, pref):
    t = min(pref, n)
    while n % t:
        t //= 2
    return t


def _pad_rows_front(a, rows):
    return jnp.pad(a, ((0, 0), (rows - a.shape[1], 0), (0, 0)))


def _lane_row(vals, offset):
    return jnp.zeros((1, LANES), F32).at[0, offset:offset + vals.shape[0]].set(vals.astype(F32))


def _layer(hs, states, wts, final_g):
    (norm1_g, w_in, sc_conv_w, dn_conv_w, a_log, dt_bias, dn_norm_g, w_br_sc, w_br_dn, w_o, norm2_g,
     peer_wq, peer_keys, peer_u, peer_v) = wts
    d = w_in.shape[0]
    sc = sc_conv_w.shape[1]
    qkv = dn_conv_w.shape[1]
    vw = DN_HEADS * DN_DV
    sc_taps, dn_taps = sc_conv_w.shape[0], dn_conv_w.shape[0]

    o_scb, o_scc, o_sch, o_qkv = 0, sc, 2 * sc, 3 * sc
    o_z = o_qkv + qkv
    o_beta = o_z + vw
    o_a = o_beta + DN_HEADS
    o_gsc = o_a + DN_HEADS
    o_gdn = o_gsc + d
    seg = lambda o, w: w_in[:, o:o + w]
    w_main = jnp.concatenate([seg(o_qkv, qkv), seg(o_z, vw), seg(o_gsc, d), seg(o_gdn, d),
                              seg(o_scb, sc), seg(o_scc, sc), seg(o_sch, sc)], axis=1).astype(BF16)
    w_main = w_main.reshape(d, w_main.shape[1] // PROJ_TILE, PROJ_TILE).transpose(1, 0, 2)
    col_z, col_gsc, col_gdn, col_sc = qkv, qkv + vw, qkv + vw + d, qkv + vw + 2 * d
    w_ba = jnp.pad(jnp.concatenate([seg(o_beta, DN_HEADS), seg(o_a, DN_HEADS)], axis=1),
                   ((0, 0), (0, LANES - 2 * DN_HEADS))).astype(BF16)
    scw = jnp.pad(sc_conv_w, ((0, SUBLANES - sc_taps), (0, 0)))
    dnw = jnp.pad(dn_conv_w, ((0, SUBLANES - dn_taps), (0, 0)))
    alog = _lane_row(a_log, DN_HEADS)
    dtb = _lane_row(dt_bias, DN_HEADS)
    ng = dn_norm_g.reshape(1, DN_DV).astype(F32)
    wsc, wdn, wo = w_br_sc.astype(BF16), w_br_dn.astype(BF16), w_o.astype(BF16)
    wqt = peer_wq.T.astype(BF16)
    keys = peer_keys.reshape(2 * PEER_HEADS, PEER_NKEYS, PEER_HALF).astype(BF16)
    u_tab = peer_u.astype(BF16)
    vt_tab = peer_v.astype(BF16).reshape(-1, EXPERT_TILE, d).transpose(0, 2, 1)
    g1 = norm1_g.reshape(1, d)
    g2 = norm2_g.reshape(1, d)
    gf = jnp.ones((1, d), F32) if final_g is None else final_g.reshape(1, d)

    outs, new_states = [], []
    for x3, (dn_state, dn_buf, sc_buf) in zip(hs, states):
        nb, t, _ = x3.shape
        n = nb * t
        x = x3.reshape(n, d)
        proj, ba = _inproj(x, g1, w_main, w_ba, _pick(n, 1024))
        ysc, q, k, v, bg, sc_new, dn_new = _prep(
            proj, ba, _pad_rows_front(sc_buf, SUBLANES), _pad_rows_front(dn_buf, SUBLANES), scw, dnw, alog, dtb,
            nb, t, _pick(t, 256), sc, qkv, col_sc, sc_taps, dn_taps)
        o, s_new = _delta(q, k, v, bg, dn_state.astype(F32), nb, t)
        h1 = _merge(x, ysc, o, proj, ng, wsc, wdn, wo, _pick(n, 256), col_z, col_gsc, col_gdn)
        xnt, fa, fb, fc, fr = _router(h1, g2, wqt, keys, _pick(n, 256))
        pack = lambda f: f.reshape(PEER_HEADS, PEER_NKEYS // BF16_ROWS, BF16_ROWS, n)
        y = _experts(xnt, u_tab, vt_tab, fa, pack(fb), fc, pack(fr), h1, gf, _pick(n, 512), final_g is not None)
        outs.append(y.reshape(nb, t, d))
        new_states.append((s_new.astype(dn_state.dtype), dn_new[:, SUBLANES - (dn_taps - 1):, :],
                           sc_new[:, SUBLANES - (sc_taps - 1):, :]))
    return outs, new_states


def kernel(x_prompt, x_sample, state_dn, state_dn_conv, state_sc_conv, norm1_g, w_in, sc_conv_w, dn_conv_w,
           dn_a_log, dn_dt_bias, dn_norm_g, w_br_sc, w_br_dn, w_o, norm2_g, peer_wq, peer_keys, peer_u, peer_v,
           final_norm_g):
    depth = w_in.shape[0]
    nbp = x_prompt.shape[0]
    hs = [x_prompt, x_sample]
    per_layer = []
    for l in range(depth):
        wts = (norm1_g[l], w_in[l], sc_conv_w[l], dn_conv_w[l], dn_a_log[l], dn_dt_bias[l], dn_norm_g[l],
               w_br_sc[l], w_br_dn[l], w_o[l], norm2_g[l], peer_wq[l], peer_keys[l], peer_u[l], peer_v[l])
        zero_states = (jnp.zeros((nbp, DN_HEADS, DN_DK, DN_DV), state_dn.dtype),
                       jnp.zeros((nbp, dn_conv_w.shape[1] - 1, dn_conv_w.shape[2]), x_prompt.dtype),
                       jnp.zeros((nbp, sc_conv_w.shape[1] - 1, sc_conv_w.shape[2]), x_prompt.dtype))
        states = [zero_states, (state_dn[l], state_dn_conv[l], state_sc_conv[l])]
        hs, new_states = _layer(hs, states, wts, final_norm_g if l == depth - 1 else None)
        per_layer.append(new_states)
    stack = lambda g, i: jnp.stack([per_layer[l][g][i] for l in range(depth)])
    return (hs[0], hs[1], stack(0, 0), stack(0, 1), stack(0, 2), stack(1, 0), stack(1, 1), stack(1, 2))
```

```python
import functools
import math

import jax
import jax.numpy as jnp
from jax import lax
from jax.experimental import pallas as pl
from jax.experimental.pallas import tpu as pltpu

F32 = jnp.float32
BF16 = jnp.bfloat16
EPS = 1e-6
LANES = 128
SUBLANES = 8
BF16_ROWS = 16
GATE_STEP = 2.0
PROJ_TILE = 1024
EXPERT_TILE = 1024
CHUNK = 64
DN_HEADS = 8
DN_DK = 128
DN_DV = 128
PEER_HEADS = 8
PEER_NKEYS = 128
PEER_HALF = 128
PEER_TOPK = 16
NEG_INF = float("-inf")
MIB = 2 ** 20


def _params(sem, vmem_mib):
    return pltpu.CompilerParams(dimension_semantics=sem, vmem_limit_bytes=vmem_mib * MIB)


def _dot(a, b):
    return jnp.dot(a.astype(BF16), b.astype(BF16), preferred_element_type=F32)


def _dot_nt(a, b):
    return lax.dot_general(a.astype(BF16), b.astype(BF16), (((1,), (1,)), ((), ())),
                           preferred_element_type=F32)


def _dot_tn(a, b):
    return lax.dot_general(a.astype(BF16), b.astype(BF16), (((0,), (0,)), ((), ())),
                           preferred_element_type=F32)


def _split(x):
    hi = x.astype(BF16)
    return hi, (x - hi.astype(F32)).astype(BF16)


def _dot3(a, b):
    ah, al = _split(a)
    bh, bl = _split(b)
    return jnp.dot(jnp.concatenate([ah, al, ah], axis=1), jnp.concatenate([bh, bh, bl], axis=0),
                   preferred_element_type=F32)


def _rms(x, g):
    return x * lax.rsqrt(jnp.mean(x * x, axis=-1, keepdims=True) + EPS) * g


def _softplus(x):
    return jnp.maximum(x, 0.0) + jnp.log1p(jnp.exp(-jnp.abs(x)))


def _inproj_body(x_ref, g_ref, w_ref, wba_ref, proj_ref, ba_ref, xn_ref):
    @pl.when(pl.program_id(1) == 0)
    def _():
        xn = _rms(x_ref[...], g_ref[...]).astype(BF16)
        xn_ref[...] = xn
        ba_ref[...] = jnp.dot(xn, wba_ref[...], preferred_element_type=F32)

    proj_ref[...] = jnp.dot(xn_ref[...], w_ref[...], preferred_element_type=F32)


def _inproj(x, g, w_main, w_ba, tm):
    n, d = x.shape
    n_tiles, _, tn = w_main.shape
    width = n_tiles * tn
    return pl.pallas_call(
        _inproj_body,
        grid=(n // tm, n_tiles),
        in_specs=[
            pl.BlockSpec((tm, d), lambda i, j: (i, 0)),
            pl.BlockSpec((1, d), lambda i, j: (0, 0)),
            pl.BlockSpec((None, d, tn), lambda i, j: (j, 0, 0)),
            pl.BlockSpec((d, LANES), lambda i, j: (0, 0)),
        ],
        out_specs=[
            pl.BlockSpec((tm, tn), lambda i, j: (i, j)),
            pl.BlockSpec((tm, LANES), lambda i, j: (i, 0)),
        ],
        out_shape=[jax.ShapeDtypeStruct((n, width), F32), jax.ShapeDtypeStruct((n, LANES), F32)],
        scratch_shapes=[pltpu.VMEM((tm, d), BF16)],
        compiler_params=_params(("parallel", "arbitrary"), 48),
        name="inproj",
    )(x, g, w_main, w_ba)


def _prep_body(scb_ref, scc_ref, sch_ref, qkv_ref, ba_ref, scbuf_ref, dnbuf_ref, scw_ref, dnw_ref,
               alog_ref, dtb_ref,
               ysc_ref, q_ref, k_ref, v_ref, bg_ref, scnew_ref, dnnew_ref,
               ext_sc, ext_dn, *, sc_taps, dn_taps):
    tt = scb_ref.shape[0]
    halo = SUBLANES

    @pl.when(pl.program_id(1) == 0)
    def _():
        ext_sc[0:halo, :] = scbuf_ref[...]
        ext_dn[0:halo, :] = dnbuf_ref[...]

    ext_sc[halo:halo + tt, :] = scc_ref[...] * sch_ref[...]
    for cg in range(scb_ref.shape[1] // LANES):
        cs = slice(cg * LANES, (cg + 1) * LANES)
        acc = None
        for i in range(sc_taps):
            r0 = halo - (sc_taps - 1) + i
            term = ext_sc[r0:r0 + tt, cs] * scw_ref[i:i + 1, cs]
            acc = term if acc is None else acc + term
        ysc_ref[:, cs] = (scb_ref[:, cs] * acc).astype(BF16)
    scnew_ref[...] = ext_sc[tt:tt + halo, :]
    ext_sc[0:halo, :] = ext_sc[tt:tt + halo, :]

    ext_dn[halo:halo + tt, :] = qkv_ref[...]
    n_qk = DN_HEADS * DN_DK // LANES
    for cg in range(qkv_ref.shape[1] // LANES):
        cs = slice(cg * LANES, (cg + 1) * LANES)
        acc = None
        for i in range(dn_taps):
            r0 = halo - (dn_taps - 1) + i
            term = ext_dn[r0:r0 + tt, cs] * dnw_ref[i:i + 1, cs]
            acc = term if acc is None else acc + term
        a = acc * jax.nn.sigmoid(acc)
        if cg < 2 * n_qk:
            a = a * lax.rsqrt(jnp.sum(a * a, axis=-1, keepdims=True) + EPS)
        if cg < n_qk:
            q_ref[:, cs] = a * (DN_DK ** -0.5)
        elif cg < 2 * n_qk:
            k_ref[:, slice((cg - n_qk) * LANES, (cg - n_qk + 1) * LANES)] = a
        else:
            v_ref[:, slice((cg - 2 * n_qk) * LANES, (cg - 2 * n_qk + 1) * LANES)] = a
    dnnew_ref[...] = ext_dn[tt:tt + halo, :]
    ext_dn[0:halo, :] = ext_dn[tt:tt + halo, :]

    ba = ba_ref[...]
    lane = lax.broadcasted_iota(jnp.int32, ba.shape, 1)
    bg_ref[...] = jnp.where(lane < DN_HEADS, jax.nn.sigmoid(ba),
                            -jnp.exp(alog_ref[...]) * _softplus(ba + dtb_ref[...]))


def _prep(proj, ba, scbuf, dnbuf, scw, dnw, alog, dtb, nb, t, tt, sc, qkv, col0_sc, sc_taps, dn_taps):
    n = nb * t
    nt = t // tt
    vw = DN_HEADS * DN_DV
    row = lambda b, s: b * nt + s
    scblk = col0_sc // sc
    body = functools.partial(_prep_body, sc_taps=sc_taps, dn_taps=dn_taps)
    return pl.pallas_call(
        body,
        grid=(nb, nt),
        in_specs=[
            pl.BlockSpec((tt, sc), lambda b, s: (row(b, s), scblk)),
            pl.BlockSpec((tt, sc), lambda b, s: (row(b, s), scblk + 1)),
            pl.BlockSpec((tt, sc), lambda b, s: (row(b, s), scblk + 2)),
            pl.BlockSpec((tt, qkv), lambda b, s: (row(b, s), 0)),
            pl.BlockSpec((tt, LANES), lambda b, s: (row(b, s), 0)),
            pl.BlockSpec((None, SUBLANES, sc), lambda b, s: (b, 0, 0)),
            pl.BlockSpec((None, SUBLANES, qkv), lambda b, s: (b, 0, 0)),
            pl.BlockSpec((SUBLANES, sc), lambda b, s: (0, 0)),
            pl.BlockSpec((SUBLANES, qkv), lambda b, s: (0, 0)),
            pl.BlockSpec((1, LANES), lambda b, s: (0, 0)),
            pl.BlockSpec((1, LANES), lambda b, s: (0, 0)),
        ],
        out_specs=[
            pl.BlockSpec((tt, sc), lambda b, s: (row(b, s), 0)),
            pl.BlockSpec((tt, vw), lambda b, s: (row(b, s), 0)),
            pl.BlockSpec((tt, vw), lambda b, s: (row(b, s), 0)),
            pl.BlockSpec((tt, vw), lambda b, s: (row(b, s), 0)),
            pl.BlockSpec((tt, LANES), lambda b, s: (row(b, s), 0)),
            pl.BlockSpec((None, SUBLANES, sc), lambda b, s: (b, 0, 0)),
            pl.BlockSpec((None, SUBLANES, qkv), lambda b, s: (b, 0, 0)),
        ],
        out_shape=[
            jax.ShapeDtypeStruct((n, sc), BF16),
            jax.ShapeDtypeStruct((n, vw), F32),
            jax.ShapeDtypeStruct((n, vw), F32),
            jax.ShapeDtypeStruct((n, vw), F32),
            jax.ShapeDtypeStruct((n, LANES), F32),
            jax.ShapeDtypeStruct((nb, SUBLANES, sc), F32),
            jax.ShapeDtypeStruct((nb, SUBLANES, qkv), F32),
        ],
        scratch_shapes=[pltpu.VMEM((tt + SUBLANES, sc), F32), pltpu.VMEM((tt + SUBLANES, qkv), F32)],
        compiler_params=_params(("parallel", "arbitrary"), 48),
        name="prep",
    )(proj, proj, proj, proj, ba, scbuf, dnbuf, scw, dnw, alog, dtb)


_INV_BASE = 8

def _delta_body(q_ref, k_ref, v_ref, bg_ref, s0_ref, o_ref, sout_ref, s_ref, low_ref, qk_ref):
    c = pl.program_id(1)
    nc = pl.num_programs(1)

    @pl.when(c == 0)
    def _():
        s_ref[...] = s0_ref[...].reshape(s_ref.shape)

    ri = lax.broadcasted_iota(jnp.int32, (CHUNK, CHUNK), 0)
    ci = lax.broadcasted_iota(jnp.int32, (CHUNK, CHUNK), 1)
    causal = ri >= ci
    strict = ri > ci
    eye = (ri == ci).astype(F32)
    base = _INV_BASE
    base_blocks = (ri // base) == (ci // base)
    levels = []
    s = base
    while s < CHUNK:
        levels.append(((ri // (2 * s)) == (ci // (2 * s))) & ((ri // s) % 2 == 1) & ((ci // s) % 2 == 0))
        s *= 2

    ns = q_ref.shape[0]
    rw = lax.broadcasted_iota(jnp.int32, (CHUNK, LANES), 0)
    spread = lambda a, l0: [jnp.broadcast_to(a[:, l0 + h:l0 + h + 1], (CHUNK, LANES)) for h in range(DN_HEADS)]
    beta_x, gc_col, gc_row = [], [], []
    for b in range(ns):
        run = bg_ref[b]
        shift = 1
        while shift < CHUNK:
            run = run + jnp.where(rw >= shift, pltpu.roll(run, shift, 0), 0.0)
            shift *= 2
        beta_x.append(spread(bg_ref[b], 0))
        gc_col.append(spread(run, DN_HEADS))
        run_t = jnp.concatenate([run, jnp.zeros((LANES - CHUNK, LANES), F32)], axis=0).T
        gc_row.append([jnp.broadcast_to(run_t[DN_HEADS + h:DN_HEADS + h + 1, :CHUNK], (CHUNK, CHUNK))
                       for h in range(DN_HEADS)])
    units = [(b, h) for b in range(ns) for h in range(DN_HEADS)]
    idx = range(len(units))
    cs = [slice(h * LANES, (h + 1) * LANES) for h in range(DN_HEADS)]
    q_of = lambda i: q_ref[units[i][0], :, cs[units[i][1]]]
    k_of = lambda i: k_ref[units[i][0], :, cs[units[i][1]]]
    v_of = lambda i: v_ref[units[i][0], :, cs[units[i][1]]]
    bx_of = lambda i: beta_x[units[i][0]][units[i][1]]
    gc_of = lambda i: gc_col[units[i][0]][units[i][1]]
    gcrow_of = lambda i: gc_row[units[i][0]][units[i][1]]
    kk = [_dot_nt(jnp.concatenate([k_of(i) * bx_of(i), q_of(i)], axis=0), k_of(i)) for i in idx]
    for i in idx:
        decay = jnp.exp(jnp.where(causal, gc_of(i)[:, :CHUNK] - gcrow_of(i), NEG_INF))
        low_ref[i] = jnp.where(strict, kk[i][:CHUNK] * decay, 0.0)
        qk_ref[i] = (kk[i][CHUNK:] * decay).astype(BF16)
    m = [jnp.where(base_blocks, -low_ref[i], 0.0) for i in idx]
    tmat = [eye + m[i] for i in idx]
    p = 1
    while 2 * p < base:
        m = [_dot3(m[i], m[i]) for i in idx]
        tm = [_dot3(tmat[i], m[i]) for i in idx]
        tmat = [tmat[i] + tm[i] for i in idx]
        p *= 2
    for lvl in levels:
        tc = [_dot3(tmat[i], jnp.where(lvl, low_ref[i], 0.0)) for i in idx]
        tct = [_dot3(tc[i], tmat[i]) for i in idx]
        tmat = [tmat[i] - tct[i] for i in idx]
    uw = []
    for i in idx:
        rhs = jnp.concatenate([v_of(i) * bx_of(i), k_of(i) * bx_of(i) * jnp.exp(gc_of(i))], axis=1)
        uw.append(_dot(tmat[i], rhs))
    ws_qs = [_dot(jnp.concatenate([uw[i][:, DN_DV:], q_of(i) * jnp.exp(gc_of(i))], axis=0), s_ref[i])
             for i in idx]
    v_new = [uw[i][:, :DN_DV] - ws_qs[i][:CHUNK] for i in idx]
    qv = [jnp.dot(qk_ref[i], v_new[i].astype(BF16), preferred_element_type=F32) for i in idx]
    for i in idx:
        o_ref[units[i][0], :, cs[units[i][1]]] = ws_qs[i][CHUNK:] + qv[i]
    for i in idx:
        gc = gc_of(i)
        g_last = gc[CHUNK - 1:CHUNK, :]
        s_ref[i] = s_ref[i] * jnp.exp(g_last) + _dot_tn(k_of(i) * jnp.exp(g_last - gc), v_new[i])

    @pl.when(c == nc - 1)
    def _():
        sout_ref[...] = s_ref[...].reshape(sout_ref.shape)


def _delta(q, k, v, bg, s0, nb, t):
    nc = t // CHUNK
    vw = DN_HEADS * DN_DV
    ns = 2 if nb % 2 == 0 else 1
    tok = pl.BlockSpec((ns, CHUNK, vw), lambda b, c: (b, c, 0))
    st = pl.BlockSpec((ns, DN_HEADS, DN_DK, DN_DV), lambda b, c: (b, 0, 0, 0))
    seq = lambda a: a.reshape(nb, t, a.shape[-1])
    o, s_new = pl.pallas_call(
        _delta_body,
        grid=(nb // ns, nc),
        in_specs=[tok, tok, tok, pl.BlockSpec((ns, CHUNK, LANES), lambda b, c: (b, c, 0)), st],
        out_specs=[tok, st],
        out_shape=[jax.ShapeDtypeStruct((nb, t, vw), F32),
                   jax.ShapeDtypeStruct((nb, DN_HEADS, DN_DK, DN_DV), F32)],
        scratch_shapes=[pltpu.VMEM((ns * DN_HEADS, DN_DK, DN_DV), F32),
                        pltpu.VMEM((ns * DN_HEADS, CHUNK, CHUNK), F32),
                        pltpu.VMEM((ns * DN_HEADS, CHUNK, CHUNK), BF16)],
        compiler_params=_params(("parallel", "arbitrary"), 32),
        name="delta",
    )(seq(q), seq(k), seq(v), seq(bg), s0)
    return o.reshape(nb * t, vw), s_new


def _merge_body(x_ref, ysc_ref, o_ref, z_ref, gsc_ref, gdn_ref, ng_ref, wsc_ref, wdn_ref, wo_ref, h_ref):
    tm = x_ref.shape[0]
    parts = []
    for h in range(DN_HEADS):
        cs = slice(h * LANES, (h + 1) * LANES)
        z = z_ref[:, cs]
        on = _rms(o_ref[:, cs], ng_ref[...]) * (z * jax.nn.sigmoid(z))
        parts.append(on.astype(BF16))
    y_dn = jnp.dot(jnp.concatenate(parts, axis=1), wdn_ref[...], preferred_element_type=F32)
    y_sc = jnp.dot(ysc_ref[...], wsc_ref[...], preferred_element_type=F32)
    merged = jax.nn.sigmoid(gsc_ref[...]) * y_sc + jax.nn.sigmoid(gdn_ref[...]) * y_dn
    h_ref[...] = x_ref[...] + jnp.dot(merged.astype(BF16), wo_ref[...], preferred_element_type=F32)


def _merge(x, ysc, o, proj, ng, wsc, wdn, wo, tm, col0_z, col0_gsc, col0_gdn):
    n, d = x.shape
    sc = ysc.shape[1]
    vw = o.shape[1]
    const = lambda shape: pl.BlockSpec(shape, lambda i: (0, 0), pipeline_mode=pl.Buffered(1))
    return pl.pallas_call(
        _merge_body,
        grid=(n // tm,),
        in_specs=[
            pl.BlockSpec((tm, d), lambda i: (i, 0)),
            pl.BlockSpec((tm, sc), lambda i: (i, 0)),
            pl.BlockSpec((tm, vw), lambda i: (i, 0)),
            pl.BlockSpec((tm, vw), lambda i: (i, col0_z // vw)),
            pl.BlockSpec((tm, d), lambda i: (i, col0_gsc // d)),
            pl.BlockSpec((tm, d), lambda i: (i, col0_gdn // d)),
            const((1, LANES)),
            const((sc, d)),
            const((vw, d)),
            const((d, d)),
        ],
        out_specs=pl.BlockSpec((tm, d), lambda i: (i, 0)),
        out_shape=jax.ShapeDtypeStruct((n, d), F32),
        compiler_params=_params(("parallel",), 48),
        name="merge",
    )(x, ysc, o, proj, proj, proj, ng, wsc, wdn, wo)


_CAND_ROWS = 10 * SUBLANES


def _tree(op, xs):
    xs = list(xs)
    while len(xs) > 1:
        xs = [op(xs[i], xs[i + 1]) for i in range(0, len(xs) - 1, 2)] + (xs[-1:] if len(xs) % 2 else [])
    return xs[0]


_NO_INDEX = float(2 ** 20)


def _any_lane(flags):
    return jnp.max(_tree(jnp.maximum, flags)) > 0.5


def _topk_ranks(scores, exact):
    nk, ln = scores[0].shape
    ng = nk // SUBLANES
    probs = range(len(scores))
    sub = lax.broadcasted_iota(jnp.int32, (SUBLANES, ln), 0).astype(F32)
    vals = [[s[g * SUBLANES:(g + 1) * SUBLANES, :] for g in range(ng)] for s in scores]
    rank = [[jnp.full((SUBLANES, ln), float(PEER_TOPK), F32) for _ in range(ng)] for _ in probs]
    base = [float(g * SUBLANES) for g in range(ng)]
    tops = [[] for _ in probs]
    for it in range(PEER_TOPK):
        m8 = [_tree(jnp.maximum, vals[p]) for p in probs]
        m = [jnp.max(m8[p], axis=0, keepdims=True) for p in probs]
        if exact:
            cand = [_tree(jnp.minimum, [jnp.where(vals[p][g] == m[p], base[g], _NO_INDEX) for g in range(ng)])
                    for p in probs]
            first = [jnp.min(cand[p] + sub, axis=0, keepdims=True) for p in probs]
            grp = [first[p] - sub for p in probs]
        for p in probs:
            tops[p].append(m[p])
            for g in range(ng):
                sel = (grp[p] == base[g]) if exact else (vals[p][g] == m[p])
                rank[p][g] = jnp.where(sel, float(it), rank[p][g])
                vals[p][g] = jnp.where(sel, NEG_INF, vals[p][g])
    out = []
    for p in probs:
        if exact:
            bad = jnp.zeros((1, ln), F32)
        else:
            ranked = _tree(jnp.add, [jnp.where(rank[p][g] < float(PEER_TOPK), 1.0, 0.0) for g in range(ng)])
            bad = jnp.where(jnp.sum(ranked, axis=0, keepdims=True) != float(PEER_TOPK), 1.0, 0.0)
        out.append((jnp.concatenate(rank[p], axis=0), jnp.concatenate(tops[p], axis=0), bad))
    return out


def _joint_select(pairs, exact):
    ln = pairs[0][0].shape[1]
    probs = range(len(pairs))
    sub = lax.broadcasted_iota(jnp.int32, (SUBLANES, ln), 0).astype(F32)
    isub = lax.broadcasted_iota(jnp.int32, (SUBLANES, ln), 0)
    pos_of = [r * PEER_TOPK + sub for r in range(SUBLANES)]
    pos_of.append(SUBLANES + sub)
    pos_of.append((SUBLANES + sub) * PEER_TOPK)
    ngr = len(pos_of)
    vals = []
    for v1, v2 in pairs:
        blocks = []
        for r in range(SUBLANES):
            cnt = min(SUBLANES, PEER_TOPK // (r + 1))
            blocks.append(jnp.where(isub < cnt, v1[r:r + 1, :] + v2[0:SUBLANES, :], NEG_INF))
        blocks.append(v1[0:1, :] + v2[SUBLANES:, :])
        blocks.append(v1[SUBLANES:, :] + v2[0:1, :])
        vals.append(blocks)
    chosen = [[jnp.zeros((SUBLANES, ln), F32) for _ in range(ngr)] for _ in probs]
    for _ in range(PEER_TOPK):
        m8 = [_tree(jnp.maximum, vals[p]) for p in probs]
        m = [jnp.max(m8[p], axis=0, keepdims=True) for p in probs]
        if exact:
            pos = [[jnp.where(vals[p][g] == m[p], pos_of[g], _NO_INDEX) for g in range(ngr)] for p in probs]
            first = [jnp.min(_tree(jnp.minimum, pos[p]), axis=0, keepdims=True) for p in probs]
        for p in probs:
            for g in range(ngr):
                sel = (pos[p][g] == first[p]) if exact else (vals[p][g] == m[p])
                chosen[p][g] = jnp.where(sel, 1.0, chosen[p][g])
                vals[p][g] = jnp.where(sel, NEG_INF, vals[p][g])
    out = []
    for p in probs:
        if exact:
            bad = jnp.zeros((1, ln), F32)
        else:
            n_chosen = jnp.sum(_tree(jnp.add, chosen[p]), axis=0, keepdims=True)
            bad = jnp.where(n_chosen != float(PEER_TOPK), 1.0, 0.0)
        out.append((jnp.concatenate(chosen[p], axis=0), bad))
    return out


def _router_body(h_ref, g_ref, wqt_ref, keys_ref, xnt_ref, a_ref, b_ref, c_ref, r_ref, qt_ref, st_ref, rk_ref, vl_ref,
                 ch_ref):
    tm = h_ref.shape[0]
    xn = _rms(h_ref[...], g_ref[...])
    xnt = xn.T.astype(BF16)
    xnt_ref[...] = xnt
    qt_ref[...] = jnp.dot(wqt_ref[...], xnt, preferred_element_type=F32).astype(BF16)

    for hp in range(2 * PEER_HEADS):
        q = qt_ref[hp * PEER_HALF:(hp + 1) * PEER_HALF, :]
        st_ref[hp] = jnp.dot(keys_ref[hp], q, preferred_element_type=F32)

    chunks = [slice(lc * LANES, (lc + 1) * LANES) for lc in range(tm // LANES)]

    def rank_step(h, carry):
        probs = [(2 * h + p, ls) for p in range(2) for ls in chunks]

        def run(exact):
            res = _topk_ranks([st_ref[hp, :, ls] for hp, ls in probs], exact)
            for (hp, ls), (rank, vals, _) in zip(probs, res):
                rk_ref[hp, :, ls] = rank
                vl_ref[hp, :, ls] = vals
            return _any_lane([bad for _, _, bad in res])

        @pl.when(run(False))
        def _():
            run(True)

        return carry

    lax.fori_loop(0, PEER_HEADS, rank_step, 0)

    def head_step(h, carry):
        tops = [(vl_ref[2 * h, :, ls], vl_ref[2 * h + 1, :, ls]) for ls in chunks]

        def run(exact):
            res = _joint_select(tops, exact)
            for ls, (chosen, _) in zip(chunks, res):
                ch_ref[:, ls] = chosen
            return _any_lane([bad for _, bad in res])

        @pl.when(run(False))
        def _():
            run(True)

        for ls, (v1, v2) in zip(chunks, tops):
            chosen = ch_ref[:, ls]
            e1 = jnp.exp(v1 - v1[0:1, :])
            e2 = jnp.exp(v2 - v2[0:1, :])
            z = jnp.zeros((1, LANES), F32)
            c_rows = []
            for r in range(SUBLANES):
                blk = chosen[r * SUBLANES:(r + 1) * SUBLANES, :]
                cnt = jnp.sum(blk, axis=0, keepdims=True)
                z = z + e1[r:r + 1, :] * jnp.sum(blk * e2[0:SUBLANES, :], axis=0, keepdims=True)
                if r == 0:
                    strip = chosen[SUBLANES * SUBLANES:SUBLANES * SUBLANES + SUBLANES, :]
                    cnt = cnt + jnp.sum(strip, axis=0, keepdims=True)
                    z = z + e1[0:1, :] * jnp.sum(strip * e2[SUBLANES:, :], axis=0, keepdims=True)
                c_rows.append(cnt)
            strip = chosen[SUBLANES * SUBLANES + SUBLANES:, :]
            z = z + e2[0:1, :] * jnp.sum(strip * e1[SUBLANES:, :], axis=0, keepdims=True)
            counts = jnp.concatenate(c_rows + [strip], axis=0)
            inv_z = 1.0 / z

            rank1 = rk_ref[2 * h, :, ls]
            rank2 = rk_ref[2 * h + 1, :, ls]
            in1 = rank1 < float(PEER_TOPK)
            in2 = rank2 < float(PEER_TOPK)
            a_key = jnp.exp(jnp.where(in1, st_ref[2 * h, :, ls] - v1[0:1, :], NEG_INF)) * inv_z
            b_key = jnp.exp(jnp.where(in2, st_ref[2 * h + 1, :, ls] - v2[0:1, :], NEG_INF))
            c_key = jnp.zeros((PEER_NKEYS, LANES), F32)
            for r in range(PEER_TOPK):
                c_key = jnp.where(rank1 == float(r), counts[r:r + 1, :], c_key)
            a_ref[h, :, ls] = a_key
            c_ref[h, :, ls] = c_key * GATE_STEP
            b_ref[h, :, ls] = b_key.astype(BF16)
            r_ref[h, :, ls] = (rank2 * GATE_STEP).astype(BF16)
        return carry

    lax.fori_loop(0, PEER_HEADS, head_step, 0)


def _router(h, g, wqt, keys, tm):
    n, d = h.shape
    qd = wqt.shape[0]
    n_hp = 2 * PEER_HEADS
    fac = lambda: pl.BlockSpec((PEER_HEADS, PEER_NKEYS, tm), lambda i: (0, 0, i))
    fac_shape = lambda dt: jax.ShapeDtypeStruct((PEER_HEADS, PEER_NKEYS, n), dt)
    return pl.pallas_call(
        _router_body,
        grid=(n // tm,),
        in_specs=[
            pl.BlockSpec((tm, d), lambda i: (i, 0)),
            pl.BlockSpec((1, d), lambda i: (0, 0)),
            pl.BlockSpec((qd, d), lambda i: (0, 0), pipeline_mode=pl.Buffered(1)),
            pl.BlockSpec((n_hp, PEER_NKEYS, PEER_HALF), lambda i: (0, 0, 0), pipeline_mode=pl.Buffered(1)),
        ],
        out_specs=[pl.BlockSpec((d, tm), lambda i: (0, i)), fac(), fac(), fac(), fac()],
        out_shape=[jax.ShapeDtypeStruct((d, n), BF16), fac_shape(F32), fac_shape(BF16), fac_shape(F32),
                   fac_shape(BF16)],
        scratch_shapes=[
            pltpu.VMEM((qd, tm), BF16),
            pltpu.VMEM((n_hp, PEER_NKEYS, tm), F32),
            pltpu.VMEM((n_hp, PEER_NKEYS, tm), F32),
            pltpu.VMEM((n_hp, PEER_TOPK, tm), F32),
            pltpu.VMEM((_CAND_ROWS, tm), F32),
        ],
        compiler_params=_params(("parallel",), 48),
        name="router",
    )(h, g, wqt, keys)


def _experts_body(xnt_ref, u_ref, vt_ref, a_ref, b_ref, c_ref, r_ref, h_ref, g_ref, y_ref, acc_ref,
                  hid0_ref, hid1_ref, w_ref, *, ne, final_norm):
    s = pl.program_id(0)
    n_i = u_ref.shape[0] // PEER_NKEYS
    tm = xnt_ref.shape[1]
    pk = b_ref.shape[2]
    zero = jnp.zeros((), BF16)

    @pl.when(s == 0)
    def _():
        acc_ref[...] = jnp.zeros_like(acc_ref)
        hid1_ref[...] = jnp.zeros_like(hid1_ref)

    def step(hid_next, hid_cur):
        hid_next[...] = jnp.dot(u_ref[...], xnt_ref[...], preferred_element_type=F32)
        for ii in range(n_i):
            gate = jnp.zeros((PEER_NKEYS // pk, pk, tm), BF16)
            for h in range(PEER_HEADS):
                a_i = jnp.broadcast_to(a_ref[h, ii:ii + 1, :], (pk, tm)).astype(BF16)
                c_i = jnp.broadcast_to(c_ref[h, ii:ii + 1, :], (pk, tm)).astype(BF16)
                gate = gate + jnp.minimum(a_i * b_ref[h], jnp.maximum(c_i - r_ref[h], zero))
            x = hid_cur[ii * PEER_NKEYS:(ii + 1) * PEER_NKEYS, :]
            act = 0.5 * x * (1.0 + lax.erf(x * (2.0 ** -0.5)))
            w_ref[ii * PEER_NKEYS:(ii + 1) * PEER_NKEYS, :] = act.astype(BF16) * gate.reshape(PEER_NKEYS, tm)
        acc_ref[...] += jnp.dot(vt_ref[...], w_ref[...], preferred_element_type=F32)

    @pl.when(s % 2 == 0)
    def _():
        step(hid0_ref, hid1_ref)

    @pl.when(s % 2 == 1)
    def _():
        step(hid1_ref, hid0_ref)

    @pl.when((s > 0) & (s % ne == 0))
    def _():
        y = h_ref[...] + acc_ref[...].T
        y_ref[...] = _rms(y, g_ref[...]) if final_norm else y
        acc_ref[...] = jnp.zeros_like(acc_ref)


def _experts(xnt, u, vt, a, b, c, r, h, g, tm, final_norm):
    d, n = xnt.shape
    ne, _, te = vt.shape
    n_i = te // PEER_NKEYS
    pk = b.shape[2]
    steps = (n // tm) * ne
    ahead = lambda s: jnp.minimum(s, steps - 1)
    cur = lambda s: jnp.maximum(s - 1, 0)
    full = lambda: pl.BlockSpec((PEER_HEADS, PEER_NKEYS // pk, pk, tm), lambda s: (0, 0, 0, cur(s) // ne))
    rows = lambda: pl.BlockSpec((PEER_HEADS, n_i, tm), lambda s: (0, cur(s) % ne, cur(s) // ne))
    return pl.pallas_call(
        functools.partial(_experts_body, ne=ne, final_norm=final_norm),
        grid=(steps + 1,),
        in_specs=[
            pl.BlockSpec((d, tm), lambda s: (0, ahead(s) // ne)),
            pl.BlockSpec((te, d), lambda s: (ahead(s) % ne, 0)),
            pl.BlockSpec((None, d, te), lambda s: (cur(s) % ne, 0, 0)),
            rows(), full(), rows(), full(),
            pl.BlockSpec((tm, d), lambda s: (cur(s) // ne, 0)),
            pl.BlockSpec((1, d), lambda s: (0, 0)),
        ],
        out_specs=pl.BlockSpec((tm, d), lambda s: (cur(s) // ne, 0)),
        out_shape=jax.ShapeDtypeStruct((n, d), F32),
        scratch_shapes=[pltpu.VMEM((d, tm), F32), pltpu.VMEM((te, tm), F32), pltpu.VMEM((te, tm), F32),
                        pltpu.VMEM((te, tm), BF16)],
        compiler_params=_params(("arbitrary",), 58),
        name="experts",
    )(xnt, u, vt, a, b, c, r, h, g)


def _pick(n, pref):
    t = min(pref, n)
    while n % t:
        t //= 2
    return t


def _pad_rows_front(a, rows):
    return jnp.pad(a, ((0, 0), (rows - a.shape[1], 0), (0, 0)))


def _lane_row(vals, offset):
    return jnp.zeros((1, LANES), F32).at[0, offset:offset + vals.shape[0]].set(vals.astype(F32))


def _layer(hs, states, wts, final_g):
    (norm1_g, w_in, sc_conv_w, dn_conv_w, a_log, dt_bias, dn_norm_g, w_br_sc, w_br_dn, w_o, norm2_g,
     peer_wq, peer_keys, peer_u, peer_v) = wts
    d = w_in.shape[0]
    sc = sc_conv_w.shape[1]
    qkv = dn_conv_w.shape[1]
    vw = DN_HEADS * DN_DV
    sc_taps, dn_taps = sc_conv_w.shape[0], dn_conv_w.shape[0]

    o_scb, o_scc, o_sch, o_qkv = 0, sc, 2 * sc, 3 * sc
    o_z = o_qkv + qkv
    o_beta = o_z + vw
    o_a = o_beta + DN_HEADS
    o_gsc = o_a + DN_HEADS
    o_gdn = o_gsc + d
    seg = lambda o, w: w_in[:, o:o + w]
    w_main = jnp.concatenate([seg(o_qkv, qkv), seg(o_z, vw), seg(o_gsc, d), seg(o_gdn, d),
                              seg(o_scb, sc), seg(o_scc, sc), seg(o_sch, sc)], axis=1).astype(BF16)
    w_main = w_main.reshape(d, w_main.shape[1] // PROJ_TILE, PROJ_TILE).transpose(1, 0, 2)
    col_z, col_gsc, col_gdn, col_sc = qkv, qkv + vw, qkv + vw + d, qkv + vw + 2 * d
    w_ba = jnp.pad(jnp.concatenate([seg(o_beta, DN_HEADS), seg(o_a, DN_HEADS)], axis=1),
                   ((0, 0), (0, LANES - 2 * DN_HEADS))).astype(BF16)
    scw = jnp.pad(sc_conv_w, ((0, SUBLANES - sc_taps), (0, 0)))
    dnw = jnp.pad(dn_conv_w, ((0, SUBLANES - dn_taps), (0, 0)))
    alog = _lane_row(a_log, DN_HEADS)
    dtb = _lane_row(dt_bias, DN_HEADS)
    ng = dn_norm_g.reshape(1, DN_DV).astype(F32)
    wsc, wdn, wo = w_br_sc.astype(BF16), w_br_dn.astype(BF16), w_o.astype(BF16)
    wqt = peer_wq.T.astype(BF16)
    keys = peer_keys.reshape(2 * PEER_HEADS, PEER_NKEYS, PEER_HALF).astype(BF16)
    u_tab = peer_u.astype(BF16)
    vt_tab = peer_v.astype(BF16).reshape(-1, EXPERT_TILE, d).transpose(0, 2, 1)
    g1 = norm1_g.reshape(1, d)
    g2 = norm2_g.reshape(1, d)
    gf = jnp.ones((1, d), F32) if final_g is None else final_g.reshape(1, d)

    outs, new_states = [], []
    for x3, (dn_state, dn_buf, sc_buf) in zip(hs, states):
        nb, t, _ = x3.shape
        n = nb * t
        x = x3.reshape(n, d)
        proj, ba = _inproj(x, g1, w_main, w_ba, _pick(n, 1024))
        ysc, q, k, v, bg, sc_new, dn_new = _prep(
            proj, ba, _pad_rows_front(sc_buf, SUBLANES), _pad_rows_front(dn_buf, SUBLANES), scw, dnw, alog, dtb,
            nb, t, _pick(t, 256), sc, qkv, col_sc, sc_taps, dn_taps)
        o, s_new = _delta(q, k, v, bg, dn_state.astype(F32), nb, t)
        h1 = _merge(x, ysc, o, proj, ng, wsc, wdn, wo, _pick(n, 256), col_z, col_gsc, col_gdn)
        xnt, fa, fb, fc, fr = _router(h1, g2, wqt, keys, _pick(n, 256))
        pack = lambda f: f.reshape(PEER_HEADS, PEER_NKEYS // BF16_ROWS, BF16_ROWS, n)
        y = _experts(xnt, u_tab, vt_tab, fa, pack(fb), fc, pack(fr), h1, gf, _pick(n, 512), final_g is not None)
        outs.append(y.reshape(nb, t, d))
        new_states.append((s_new.astype(dn_state.dtype), dn_new[:, SUBLANES - (dn_taps - 1):, :],
                           sc_new[:, SUBLANES - (sc_taps - 1):, :]))
    return outs, new_states


def kernel(x_prompt, x_sample, state_dn, state_dn_conv, state_sc_conv, norm1_g, w_in, sc_conv_w, dn_conv_w,
           dn_a_log, dn_dt_bias, dn_norm_g, w_br_sc, w_br_dn, w_o, norm2_g, peer_wq, peer_keys, peer_u, peer_v,
           final_norm_g):
    depth = w_in.shape[0]
    nbp = x_prompt.shape[0]
    hs = [x_prompt, x_sample]
    per_layer = []
    for l in range(depth):
        wts = (norm1_g[l], w_in[l], sc_conv_w[l], dn_conv_w[l], dn_a_log[l], dn_dt_bias[l], dn_norm_g[l],
               w_br_sc[l], w_br_dn[l], w_o[l], norm2_g[l], peer_wq[l], peer_keys[l], peer_u[l], peer_v[l])
        zero_states = (jnp.zeros((nbp, DN_HEADS, DN_DK, DN_DV), state_dn.dtype),
                       jnp.zeros((nbp, dn_conv_w.shape[1] - 1, dn_conv_w.shape[2]), x_prompt.dtype),
                       jnp.zeros((nbp, sc_conv_w.shape[1] - 1, sc_conv_w.shape[2]), x_prompt.dtype))
        states = [zero_states, (state_dn[l], state_dn_conv[l], state_sc_conv[l])]
        hs, new_states = _layer(hs, states, wts, final_norm_g if l == depth - 1 else None)
        per_layer.append(new_states)
    stack = lambda g, i: jnp.stack([per_layer[l][g][i] for l in range(depth)])
    return (hs[0], hs[1], stack(0, 0), stack(0, 1), stack(0, 2), stack(1, 0), stack(1, 1), stack(1, 2))
```
